```python
import jax, jax.numpy as jnp
from jax import lax
import numpy as np

D_MODEL = 1024
BATCH = 4
SEQ = 4096
DEPTH = 4
DEC_BATCH = 128
DEC_SEQ = 1
PAST_LEN = 8192
PAGE_SIZE = 128

N_HEADS = 16
QK_NOPE = 64
QK_ROPE = 32
V_HEAD = 64
Q_LORA = 384
KV_LORA = 256
ROPE_THETA = 10000.0
ATTN_SCALE = (QK_NOPE + QK_ROPE) ** -0.5
Q_BLOCK = 128
POOL_WINDOWS = (2, 4, 8, 16)
N_POOL_GROUPS = len(POOL_WINDOWS)
POOL_GROUP_WIDTH = D_MODEL // N_POOL_GROUPS
POOL_CTX = max(POOL_WINDOWS) - 1
N_EXPERTS = 16
N_EXPERT_GROUPS = 4
EXPERTS_PER_GROUP = N_EXPERTS // N_EXPERT_GROUPS
TOP_K = 2
D_EXPERT = 512
N_MIXERS = 2
N_MLA_LAYERS = (DEPTH + 1) // 2
N_POOL_LAYERS = DEPTH // 2
ALPHA = (2 * DEPTH) ** 0.25
BETA = (8 * DEPTH) ** -0.25
LN_EPS = 1e-5
RMS_EPS = 1e-6

kernel_name = 'mla_pool_hybrid_shared_router_moe_deepnorm_step'


def layer_norm(x, g, b):
    xf = x.astype(jnp.float32)
    mu = xf.mean(-1, keepdims=True)
    var = jnp.square(xf - mu).mean(-1, keepdims=True)
    return ((xf - mu) * lax.rsqrt(var + LN_EPS) * g.astype(jnp.float32) + b.astype(jnp.float32)).astype(x.dtype)


def rms_norm(x, g):
    xf = x.astype(jnp.float32)
    return (xf * lax.rsqrt(jnp.mean(xf * xf, -1, keepdims=True) + RMS_EPS) * g.astype(jnp.float32)).astype(x.dtype)


def rope(x, pos):
    half = QK_ROPE // 2
    inv = ROPE_THETA ** (-2.0 * jnp.arange(half, dtype=jnp.float32) / QK_ROPE)
    ang = pos.astype(jnp.float32)[:, None] * inv[None, :]
    cos, sin = jnp.cos(ang)[:, None, :], jnp.sin(ang)[:, None, :]
    x1, x2 = x[..., :half].astype(jnp.float32), x[..., half:].astype(jnp.float32)
    return jnp.concatenate([x1 * cos - x2 * sin, x1 * sin + x2 * cos], -1).astype(x.dtype)


def mla_project(x, pos, w_dq, g_q, w_uq, w_dkv, g_kv):
    B, S, _ = x.shape
    c_q = rms_norm(x @ w_dq, g_q)
    q = (c_q @ w_uq).reshape(B, S, N_HEADS, QK_NOPE + QK_ROPE)
    q_nope, q_rope = q[..., :QK_NOPE], rope(q[..., QK_NOPE:], pos)
    kv = x @ w_dkv
    c_kv = rms_norm(kv[..., :KV_LORA], g_kv)
    k_rope = rope(kv[..., None, KV_LORA:], pos)[:, :, 0]
    return q_nope, q_rope, c_kv, k_rope


def mla_attend_prompt(q_nope, q_rope, c_kv, k_rope, w_uk, w_uv):
    B, S, H, _ = q_nope.shape
    k_nope = jnp.einsum('bsr,rhd->bshd', c_kv, w_uk)
    v = jnp.einsum('bsr,rhd->bshd', c_kv, w_uv)
    nb = S // Q_BLOCK
    qn = q_nope.reshape(B, nb, Q_BLOCK, H, QK_NOPE).transpose(1, 0, 2, 3, 4)
    qr = q_rope.reshape(B, nb, Q_BLOCK, H, QK_ROPE).transpose(1, 0, 2, 3, 4)
    kpos = jnp.arange(S)

    def block(args):
        i, qn_b, qr_b = args
        s = jnp.einsum('bqhd,bkhd->bhqk', qn_b, k_nope) + jnp.einsum('bqhe,bke->bhqk', qr_b, k_rope)
        qpos = i * Q_BLOCK + jnp.arange(Q_BLOCK)
        s = jnp.where(kpos[None, :] <= qpos[:, None], s.astype(jnp.float32) * ATTN_SCALE, -jnp.inf)
        p = jax.nn.softmax(s, axis=-1).astype(v.dtype)
        return jnp.einsum('bhqk,bkhd->bqhd', p, v)

    o = lax.map(block, (jnp.arange(nb), qn, qr))
    return o.transpose(1, 0, 2, 3, 4).reshape(B, S, H * V_HEAD)


def mla_attend_sample(q_nope, q_rope, c_kv, k_rope, ckv_pages, kr_pages, page_table, w_uk, w_uv):
    DB, T, H, _ = q_nope.shape
    ckv_past = ckv_pages[page_table].reshape(DB, -1, KV_LORA)
    kr_past = kr_pages[page_table].reshape(DB, -1, QK_ROPE)
    past = ckv_past.shape[1]
    q_lat = jnp.einsum('bqhd,rhd->bqhr', q_nope, w_uk)
    s_past = jnp.einsum('bqhr,bkr->bhqk', q_lat, ckv_past) + jnp.einsum('bqhe,bke->bhqk', q_rope, kr_past)
    s_new = jnp.einsum('bqhr,bkr->bhqk', q_lat, c_kv) + jnp.einsum('bqhe,bke->bhqk', q_rope, k_rope)
    causal = jnp.arange(T)[None, :] <= jnp.arange(T)[:, None]
    s_new = jnp.where(causal, s_new.astype(jnp.float32) * ATTN_SCALE, -jnp.inf)
    s = jnp.concatenate([s_past.astype(jnp.float32) * ATTN_SCALE, s_new], -1)
    p = jax.nn.softmax(s, axis=-1).astype(c_kv.dtype)
    o_lat = jnp.einsum('bhqk,bkr->bqhr', p[..., :past], ckv_past) + jnp.einsum('bhqk,bkr->bqhr', p[..., past:], c_kv)
    return jnp.einsum('bqhr,rhd->bqhd', o_lat, w_uv).reshape(DB, T, H * V_HEAD)


def pool_mix(ctx, start_pos, w_pool, pool_scale):
    B, L, D = ctx.shape
    n = L - POOL_CTX
    xf = ctx.astype(jnp.float32)
    csum = jnp.concatenate([jnp.zeros((B, 1, D), jnp.float32), jnp.cumsum(xf, axis=1)], axis=1)
    pos = start_pos + jnp.arange(n)
    end = csum[:, POOL_CTX + 1:]
    means = []
    for g, w in enumerate(POOL_WINDOWS):
        c0, c1 = g * POOL_GROUP_WIDTH, (g + 1) * POOL_GROUP_WIDTH
        begin = csum[:, POOL_CTX + 1 - w: POOL_CTX + 1 - w + n, c0:c1]
        cnt = jnp.minimum(pos + 1, w).astype(jnp.float32)[None, :, None]
        means.append((end[..., c0:c1] - begin) / cnt)
    pooled = (jnp.concatenate(means, -1) - xf[:, POOL_CTX:]).astype(ctx.dtype)
    pooled = pooled.reshape(B, n, N_POOL_GROUPS, POOL_GROUP_WIDTH)
    y = jnp.einsum('bsgc,gcd->bsgd', pooled, w_pool).reshape(B, n, D)
    return y * pool_scale


def route(t, w_router, router_bias):
    s = jax.nn.sigmoid((t @ w_router).astype(jnp.float32))
    biased = (s + router_bias.astype(jnp.float32)).reshape(-1, N_EXPERT_GROUPS, EXPERTS_PER_GROUP)
    group_score = lax.top_k(biased, TOP_K)[0].sum(-1)
    best = jnp.argmax(group_score, -1)
    in_group = jnp.arange(N_EXPERT_GROUPS)[None, :] == best[:, None]
    masked = jnp.where(in_group[..., None], biased, -jnp.inf).reshape(-1, N_EXPERTS)
    _, idx = lax.top_k(masked, TOP_K)
    w = jnp.take_along_axis(s, idx, -1)
    w = w / w.sum(-1, keepdims=True)
    return (jax.nn.one_hot(idx, N_EXPERTS, dtype=jnp.float32) * w[..., None]).sum(1)


def moe(x, w_router, router_bias, w_gate, w_up, w_down):
    B, S, D = x.shape
    t = x.reshape(B * S, D)
    combine = route(t, w_router, router_bias).astype(x.dtype)
    h = jax.nn.silu(jnp.einsum('td,edf->tef', t, w_gate)) * jnp.einsum('td,edf->tef', t, w_up)
    y = jnp.einsum('tef,efd->td', h * combine[:, :, None], w_down)
    return y.reshape(B, S, D)


def setup_inputs(seed: int = 0) -> dict:
    key = jax.random.key(seed)
    ks = jax.random.split(key, 26)
    f32 = jnp.float32
    n_pages = PAST_LEN // PAGE_SIZE
    used = DEC_BATCH * n_pages
    n_phys = used + max(1, used // 4)

    def nrm(k, shape, fan_in, gain=1.0):
        return jax.random.normal(k, shape, f32) * (gain * fan_in ** -0.5)

    def gain(k, shape):
        return 1.0 + 0.02 * jax.random.normal(k, shape, f32)

    page_table = jax.random.permutation(ks[5], n_phys)[:used].reshape(DEC_BATCH, n_pages).astype(jnp.int32)
    return {
        'x_prompt': jax.random.normal(ks[0], (BATCH, SEQ, D_MODEL), f32),
        'x_sample': jax.random.normal(ks[1], (DEC_BATCH, DEC_SEQ, D_MODEL), f32),
        'cache_ckv': jax.random.normal(ks[2], (N_MLA_LAYERS, n_phys, PAGE_SIZE, KV_LORA), f32),
        'cache_krope': jax.random.normal(ks[3], (N_MLA_LAYERS, n_phys, PAGE_SIZE, QK_ROPE), f32),
        'state_pool': jax.random.normal(ks[4], (N_POOL_LAYERS, DEC_BATCH, POOL_CTX, D_MODEL), f32),
        'page_table': page_table,
        'w_dq': nrm(ks[6], (N_MLA_LAYERS, D_MODEL, Q_LORA), D_MODEL),
        'g_q': gain(ks[7], (N_MLA_LAYERS, Q_LORA)),
        'w_uq': nrm(ks[8], (N_MLA_LAYERS, Q_LORA, N_HEADS * (QK_NOPE + QK_ROPE)), Q_LORA),
        'w_dkv': nrm(ks[9], (N_MLA_LAYERS, D_MODEL, KV_LORA + QK_ROPE), D_MODEL),
        'g_kv': gain(ks[10], (N_MLA_LAYERS, KV_LORA)),
        'w_uk': nrm(ks[11], (N_MLA_LAYERS, KV_LORA, N_HEADS, QK_NOPE), KV_LORA),
        'w_uv': nrm(ks[12], (N_MLA_LAYERS, KV_LORA, N_HEADS, V_HEAD), KV_LORA, BETA),
        'w_o': nrm(ks[13], (N_MLA_LAYERS, N_HEADS * V_HEAD, D_MODEL), N_HEADS * V_HEAD, BETA),
        'w_pool': nrm(ks[14], (N_POOL_LAYERS, N_POOL_GROUPS, POOL_GROUP_WIDTH, POOL_GROUP_WIDTH), POOL_GROUP_WIDTH, BETA),
        'pool_scale': gain(ks[15], (N_POOL_LAYERS, D_MODEL)),
        'w_router': nrm(ks[16], (D_MODEL, N_EXPERTS), D_MODEL),
        'router_bias': 0.01 * jax.random.normal(ks[17], (N_EXPERTS,), f32),
        'w_gate': nrm(ks[18], (DEPTH, N_EXPERTS, D_MODEL, D_EXPERT), D_MODEL),
        'w_up': nrm(ks[19], (DEPTH, N_EXPERTS, D_MODEL, D_EXPERT), D_MODEL, BETA),
        'w_down': nrm(ks[20], (DEPTH, N_EXPERTS, D_EXPERT, D_MODEL), D_EXPERT, BETA),
        'ln1_g': gain(ks[21], (DEPTH, D_MODEL)),
        'ln1_b': 0.02 * jax.random.normal(ks[22], (DEPTH, D_MODEL), f32),
        'ln2_g': gain(ks[23], (DEPTH, D_MODEL)),
        'ln2_b': 0.02 * jax.random.normal(ks[24], (DEPTH, D_MODEL), f32),
    }


def reference(x_prompt, x_sample, cache_ckv, cache_krope, state_pool, page_table,
              w_dq, g_q, w_uq, w_dkv, g_kv, w_uk, w_uv, w_o, w_pool, pool_scale,
              w_router, router_bias, w_gate, w_up, w_down, ln1_g, ln1_b, ln2_g, ln2_b):
    B, S, _ = x_prompt.shape
    DB, T, _ = x_sample.shape
    past_len = page_table.shape[1] * cache_ckv.shape[2]
    pos_p = jnp.arange(S)
    pos_s = past_len + jnp.arange(T)
    xp, xs = x_prompt, x_sample
    ckv_p, kr_p, pool_p, ckv_s, kr_s, pool_s = [], [], [], [], [], []
    for i in range(DEPTH):
        j = i // N_MIXERS
        if i % N_MIXERS == 0:
            qn, qr, c, k = mla_project(xp, pos_p, w_dq[j], g_q[j], w_uq[j], w_dkv[j], g_kv[j])
            mix_p = mla_attend_prompt(qn, qr, c, k, w_uk[j], w_uv[j]) @ w_o[j]
            ckv_p.append(c)
            kr_p.append(k)
            qn, qr, c, k = mla_project(xs, pos_s, w_dq[j], g_q[j], w_uq[j], w_dkv[j], g_kv[j])
            mix_s = mla_attend_sample(qn, qr, c, k, cache_ckv[j], cache_krope[j], page_table, w_uk[j], w_uv[j]) @ w_o[j]
            ckv_s.append(c)
            kr_s.append(k)
        else:
            ctx_p = jnp.concatenate([jnp.zeros((B, POOL_CTX, D_MODEL), xp.dtype), xp], axis=1)
            ctx_s = jnp.concatenate([state_pool[j].astype(xs.dtype), xs], axis=1)
            mix_p = pool_mix(ctx_p, 0, w_pool[j], pool_scale[j])
            mix_s = pool_mix(ctx_s, past_len, w_pool[j], pool_scale[j])
            pool_p.append(ctx_p[:, -POOL_CTX:])
            pool_s.append(ctx_s[:, -POOL_CTX:])
        xp = layer_norm(ALPHA * xp + mix_p, ln1_g[i], ln1_b[i])
        xs = layer_norm(ALPHA * xs + mix_s, ln1_g[i], ln1_b[i])
        xp = layer_norm(ALPHA * xp + moe(xp, w_router, router_bias, w_gate[i], w_up[i], w_down[i]), ln2_g[i], ln2_b[i])
        xs = layer_norm(ALPHA * xs + moe(xs, w_router, router_bias, w_gate[i], w_up[i], w_down[i]), ln2_g[i], ln2_b[i])
    return (xp, xs, jnp.stack(ckv_p), jnp.stack(kr_p), jnp.stack(pool_p), jnp.stack(ckv_s), jnp.stack(kr_s), jnp.stack(pool_s))
```

```python
import functools

import jax
import jax.numpy as jnp
from jax import lax
from jax.experimental import pallas as pl
from jax.experimental.pallas import tpu as pltpu

F32 = jnp.float32
BF16 = jnp.bfloat16

N_HEADS = 16
QK_NOPE = 64
QK_ROPE = 32
V_HEAD = 64
KV_LORA = 256
ROPE_THETA = 10000.0
ATTN_SCALE = (QK_NOPE + QK_ROPE) ** -0.5
POOL_WINDOWS = (2, 4, 8, 16)
POOL_CTX = max(POOL_WINDOWS) - 1
N_EXPERTS = 16
N_EXPERT_GROUPS = 4
EXPERTS_PER_GROUP = N_EXPERTS // N_EXPERT_GROUPS
TOP_K = 2
LN_EPS = 1e-5
RMS_EPS = 1e-6

LANES = 128
HALF_ROPE = QK_ROPE // 2
ROPE_LANE0 = QK_NOPE
HALO = 16
NEG_BIG = -1e30
VMEM_LIMIT = 48 * 1024 * 1024
MOE_TILE = 256
ATTN_TILE = 512
POOL_TILE = 512
PAGES_PER_STEP = 16


def _params(*sem):
    return pltpu.CompilerParams(dimension_semantics=sem, vmem_limit_bytes=VMEM_LIMIT)


def _rms(x, g):
    return x * lax.rsqrt(jnp.mean(x * x, -1, keepdims=True) + RMS_EPS) * g


def _layer_norm(y, g, b):
    mu = jnp.mean(y, -1, keepdims=True)
    d = y - mu
    var = jnp.mean(d * d, -1, keepdims=True)
    return d * lax.rsqrt(var + LN_EPS) * g + b


def _rope_chunk(c, cos, sin_up, sin_dn):
    return c * cos + pltpu.roll(c, HALF_ROPE, 1) * sin_up + pltpu.roll(c, LANES - HALF_ROPE, 1) * sin_dn


def _mla_proj_kernel(x_ref, wdq_ref, gq_ref, wuq_ref, wdkv_ref, gkv_ref, wuk_ref, wuv_ref,
                     cos_ref, sup_ref, sdn_ref, q_ref, k_ref, v_ref, ckv_ref, kr_ref):
    xb = x_ref[...].astype(BF16)
    cos, sup, sdn = cos_ref[...], sup_ref[...], sdn_ref[...]
    cq = _rms(jnp.dot(xb, wdq_ref[...], preferred_element_type=F32), gq_ref[...])
    q = jnp.dot(cq.astype(BF16), wuq_ref[...], preferred_element_type=F32)
    kv = jnp.dot(xb, wdkv_ref[...], preferred_element_type=F32)
    ckv = _rms(kv[:, :KV_LORA], gkv_ref[...])
    ckv_ref[...] = ckv
    kr = _rope_chunk(kv[:, KV_LORA:], cos, sup, sdn)
    kr_ref[...] = kr
    ckv_b = ckv.astype(BF16)
    kn = jnp.dot(ckv_b, wuk_ref[...], preferred_element_type=F32)
    v_ref[...] = jnp.dot(ckv_b, wuv_ref[...], preferred_element_type=F32).astype(BF16)
    for h in range(N_HEADS):
        sl = slice(h * LANES, (h + 1) * LANES)
        q_ref[:, sl] = (_rope_chunk(q[:, sl], cos, sup, sdn) * ATTN_SCALE).astype(BF16)
        k_ref[:, sl] = (kn[:, sl] + kr).astype(BF16)


def _mla_proj(x, wdq, gq, wuq, wdkv, gkv, wuk, wuv, cos, sup, sdn, tm):
    n, d = x.shape
    full = lambda a: pl.BlockSpec(a.shape, lambda i: (0,) * a.ndim)
    rows = lambda w: pl.BlockSpec((tm, w), lambda i: (i, 0))
    hp = N_HEADS * LANES
    return pl.pallas_call(
        _mla_proj_kernel,
        grid=(n // tm,),
        in_specs=[rows(d), full(wdq), full(gq), full(wuq), full(wdkv), full(gkv), full(wuk), full(wuv),
                  rows(LANES), rows(LANES), rows(LANES)],
        out_specs=[rows(hp), rows(hp), rows(N_HEADS * V_HEAD), rows(KV_LORA), rows(LANES)],
        out_shape=[jax.ShapeDtypeStruct((n, hp), BF16), jax.ShapeDtypeStruct((n, hp), BF16),
                   jax.ShapeDtypeStruct((n, N_HEADS * V_HEAD), BF16),
                   jax.ShapeDtypeStruct((n, KV_LORA), F32), jax.ShapeDtypeStruct((n, LANES), F32)],
        compiler_params=_params("parallel"),
        name="mla_proj",
    )(x, wdq, gq, wuq, wdkv, gkv, wuk, wuv, cos, sup, sdn)


def _flash_kernel(q_ref, k_ref, v_ref, o_ref, *, tile):
    qi = pl.program_id(2)
    nt = (((1,), (1,)), ((), ()))
    row = lax.broadcasted_iota(jnp.int32, (tile, tile), 0)
    col = lax.broadcasted_iota(jnp.int32, (tile, tile), 1)
    outs = []
    for hh in range(2):
        hs = slice(hh * LANES, (hh + 1) * LANES)
        q = q_ref[:, hs]

        def step(kv, carry, masked):
            m, l, acc = carry
            start = pl.multiple_of(kv * tile, tile)
            s = lax.dot_general(q, k_ref[pl.ds(start, tile), hs], nt, preferred_element_type=F32)
            if masked:
                s = jnp.where(col <= row, s, NEG_BIG)
            m_new = jnp.maximum(m, jnp.max(s, -1, keepdims=True))
            p = jnp.exp(s - m_new)
            a = jnp.exp(m - m_new)
            l = a * l + jnp.sum(p, -1, keepdims=True)
            acc = a * acc + jnp.dot(p.astype(BF16), v_ref[pl.ds(start, tile), :], preferred_element_type=F32)
            return m_new, l, acc

        init = (jnp.full((tile, 1), NEG_BIG, F32), jnp.zeros((tile, 1), F32), jnp.zeros((tile, LANES), F32))
        carry = lax.fori_loop(0, qi, lambda kv, c: step(kv, c, False), init)
        _, l, acc = step(qi, carry, True)
        outs.append(acc / l)
    lane = lax.broadcasted_iota(jnp.int32, (tile, LANES), 1)
    o_ref[...] = jnp.where(lane < V_HEAD, outs[0], outs[1]).astype(BF16)


def _flash_attention(q, k, v, batch, seq, tile):
    n = q.shape[0]
    nq = seq // tile
    pairs = N_HEADS // 2
    return pl.pallas_call(
        functools.partial(_flash_kernel, tile=tile),
        grid=(batch, pairs, nq),
        in_specs=[pl.BlockSpec((tile, 2 * LANES), lambda b, p, i: (b * nq + i, p)),
                  pl.BlockSpec((seq, 2 * LANES), lambda b, p, i: (b, p)),
                  pl.BlockSpec((seq, 2 * V_HEAD), lambda b, p, i: (b, p))],
        out_specs=pl.BlockSpec((tile, 2 * V_HEAD), lambda b, p, i: (b * nq + i, p)),
        out_shape=jax.ShapeDtypeStruct((n, N_HEADS * V_HEAD), BF16),
        compiler_params=_params("parallel", "parallel", "arbitrary"),
        name="flash_prompt",
    )(q, k, v)


def _qlat_kernel(q_ref, w_ref, o_ref):
    o_ref[...] = jnp.dot(q_ref[...], w_ref[...], preferred_element_type=F32).astype(BF16)


def _q_latent(q, w_ukt, db):
    n = q.shape[0]
    blk = n // db - 1
    return pl.pallas_call(
        _qlat_kernel,
        grid=(N_HEADS,),
        in_specs=[pl.BlockSpec((db, LANES), lambda h: (blk, h)),
                  pl.BlockSpec((None, LANES, KV_LORA), lambda h: (h, 0, 0))],
        out_specs=pl.BlockSpec((None, db, KV_LORA), lambda h: (h, 0, 0)),
        out_shape=jax.ShapeDtypeStruct((N_HEADS, db, KV_LORA), BF16),
        compiler_params=_params("parallel"),
        name="q_latent",
    )(q, w_ukt)


def _paged_kernel(pt_ref, ql_ref, qr_ref, cnew_ref, knew_ref, *rest, pps):
    ckv_refs = rest[:pps]
    kr_refs = rest[pps:2 * pps]
    o_ref = rest[2 * pps]
    m_scr, l_scr, acc_scr = rest[2 * pps + 1:]
    c = pl.program_id(1)
    nt = (((1,), (1,)), ((), ()))

    @pl.when(c == 0)
    def _():
        m_scr[...] = jnp.full(m_scr.shape, NEG_BIG, F32)
        l_scr[...] = jnp.zeros(l_scr.shape, F32)
        acc_scr[...] = jnp.zeros(acc_scr.shape, F32)

    ql = ql_ref[...]
    qr = qr_ref[...]
    pages = [r[...].astype(BF16) for r in ckv_refs]
    s = jnp.concatenate(
        [lax.dot_general(ql, pages[i], nt, preferred_element_type=F32)
         + lax.dot_general(qr, kr_refs[i][...].astype(BF16), nt, preferred_element_type=F32)
         for i in range(pps)], axis=1)
    m = m_scr[...]
    m_new = jnp.maximum(m, jnp.max(s, -1, keepdims=True))
    p = jnp.exp(s - m_new).astype(BF16)
    a = jnp.exp(m - m_new)
    psz = pages[0].shape[0]
    pv = jnp.dot(p[:, :psz], pages[0], preferred_element_type=F32)
    for i in range(1, pps):
        pv = pv + jnp.dot(p[:, i * psz:(i + 1) * psz], pages[i], preferred_element_type=F32)
    l_new = a * l_scr[...] + jnp.sum(p.astype(F32), -1, keepdims=True)
    acc_new = a * acc_scr[...] + pv
    m_scr[...] = m_new
    l_scr[...] = l_new
    acc_scr[...] = acc_new

    @pl.when(c == pl.num_programs(1) - 1)
    def _():
        cn = cnew_ref[...].astype(BF16).astype(F32)
        kn = knew_ref[...].astype(BF16).astype(F32)
        s_self = (jnp.sum(ql.astype(F32) * cn, -1, keepdims=True)
                  + jnp.sum(qr.astype(F32) * kn, -1, keepdims=True))
        m_f = jnp.maximum(m_new, s_self)
        a_f = jnp.exp(m_new - m_f)
        p_self = jnp.exp(s_self - m_f).astype(BF16).astype(F32)
        o_ref[...] = (a_f * acc_new + p_self * cn) / (a_f * l_new + p_self)


def _paged_attention(page_table, q_lat, q_rope, c_new, k_new, cache_ckv, cache_krope, layer, pps):
    db, n_pages = page_table.shape
    psz = cache_ckv.shape[2]
    n_chunks = n_pages // pps
    pt = page_table.reshape(-1)

    def page_spec(width, i):
        return pl.BlockSpec((None, None, psz, width),
                            lambda b, c, pt_ref: (layer, pt_ref[b * n_pages + c * pps + i], 0, 0))

    per_b = lambda r, w: pl.BlockSpec((None, r, w), lambda b, c, pt_ref: (b, 0, 0))
    grid_spec = pltpu.PrefetchScalarGridSpec(
        num_scalar_prefetch=1,
        grid=(db, n_chunks),
        in_specs=[per_b(N_HEADS, KV_LORA), per_b(N_HEADS, QK_ROPE), per_b(1, KV_LORA), per_b(1, QK_ROPE)]
                 + [page_spec(KV_LORA, i) for i in range(pps)]
                 + [page_spec(QK_ROPE, i) for i in range(pps)],
        out_specs=per_b(N_HEADS, KV_LORA),
        scratch_shapes=[pltpu.VMEM((N_HEADS, 1), F32), pltpu.VMEM((N_HEADS, 1), F32),
                        pltpu.VMEM((N_HEADS, KV_LORA), F32)],
    )
    return pl.pallas_call(
        functools.partial(_paged_kernel, pps=pps),
        grid_spec=grid_spec,
        out_shape=jax.ShapeDtypeStruct((db, N_HEADS, KV_LORA), F32),
        compiler_params=_params("parallel", "arbitrary"),
        name="paged_sample",
    )(pt, q_lat, q_rope, c_new, k_new, *([cache_ckv] * pps), *([cache_krope] * pps))


def _sample_out_kernel(ol_ref, w_ref, oin_ref, o_ref):
    del oin_ref
    acc = jnp.dot(ol_ref[0].astype(BF16), w_ref[0], preferred_element_type=F32)
    acc = acc + jnp.dot(ol_ref[1].astype(BF16), w_ref[1], preferred_element_type=F32)
    o_ref[...] = acc.astype(BF16)


def _sample_out(o_lat_t, w_uv_pairs, o_all, db):
    n = o_all.shape[0]
    blk = n // db - 1
    return pl.pallas_call(
        _sample_out_kernel,
        grid=(N_HEADS // 2,),
        in_specs=[pl.BlockSpec((2, db, KV_LORA), lambda p: (p, 0, 0)),
                  pl.BlockSpec((2, KV_LORA, 2 * V_HEAD), lambda p: (p, 0, 0)),
                  pl.BlockSpec(memory_space=pl.ANY)],
        out_specs=pl.BlockSpec((db, 2 * V_HEAD), lambda p: (blk, p)),
        out_shape=jax.ShapeDtypeStruct(o_all.shape, o_all.dtype),
        input_output_aliases={2: 0},
        compiler_params=_params("parallel"),
        name="sample_out",
    )(o_lat_t, w_uv_pairs, o_all)


def _pool_prompt_kernel(x_ref, halo_ref, o_ref, l0, l1, l2, l3, *, tile, seq):
    i = pl.program_id(0)
    d = x_ref.shape[1]
    gw = d // len(POOL_WINDOWS)
    ext = tile + HALO
    s0 = (i * tile) % seq
    pos_ext = s0 - HALO + lax.broadcasted_iota(jnp.int32, (ext, 1), 0)
    zeros = jnp.zeros((HALO, d), F32)
    for buf in (l0, l1, l2, l3):
        buf[0:HALO, :] = zeros
    l0[HALO:2 * HALO, :] = halo_ref[...]
    l0[2 * HALO:, :] = x_ref[...]

    def level(src, dst, k, c0):
        cur = src[HALO:HALO + ext, c0:]
        sh = src[HALO - k:HALO - k + ext, c0:]
        dst[HALO:HALO + ext, c0:] = cur + jnp.where(pos_ext >= k, sh, 0.0)

    level(l0, l1, 1, 0)
    level(l1, l2, 2, gw)
    level(l2, l3, 4, 2 * gw)
    pos = pos_ext[HALO:]
    x = x_ref[...]
    t0 = 2 * HALO
    sums = (l1[t0:, 0:gw], l2[t0:, gw:2 * gw], l3[t0:, 2 * gw:3 * gw],
            l3[t0:, 3 * gw:] + jnp.where(pos >= 8, l3[t0 - 8:t0 - 8 + tile, 3 * gw:], 0.0))
    for g, w in enumerate(POOL_WINDOWS):
        cnt = jnp.minimum(pos + 1, w).astype(F32)
        o_ref[:, g * gw:(g + 1) * gw] = (sums[g] / cnt - x[:, g * gw:(g + 1) * gw]).astype(BF16)


def _pool_prompt(x, batch, seq, tile):
    n, d = x.shape
    hb = tile // HALO
    return pl.pallas_call(
        functools.partial(_pool_prompt_kernel, tile=tile, seq=seq),
        grid=(batch * seq // tile,),
        in_specs=[pl.BlockSpec((tile, d), lambda i: (i, 0)),
                  pl.BlockSpec((HALO, d), lambda i: (jnp.maximum(i * hb - 1, 0), 0))],
        out_specs=pl.BlockSpec((tile, d), lambda i: (i, 0)),
        out_shape=jax.ShapeDtypeStruct((n, d), BF16),
        scratch_shapes=[pltpu.VMEM((tile + 2 * HALO, d), F32)] * 4,
        compiler_params=_params("parallel"),
        name="pool_prompt",
    )(x, x)


def _pool_sample_kernel(st_ref, x_ref, pin_ref, o_ref):
    del pin_ref
    d = x_ref.shape[1]
    gw = d // len(POOL_WINDOWS)
    x = x_ref[...]
    for g, w in enumerate(POOL_WINDOWS):
        cs = slice(g * gw, (g + 1) * gw)
        acc = x[:, cs]
        for r in range(POOL_CTX - (w - 1), POOL_CTX):
            acc = acc + st_ref[:, r, cs]
        o_ref[:, cs] = (acc / float(w) - x[:, cs]).astype(BF16)


def _pool_sample(state, x, pooled, db):
    n, d = x.shape
    blk = n // db - 1
    return pl.pallas_call(
        _pool_sample_kernel,
        grid=(1,),
        in_specs=[pl.BlockSpec(state.shape, lambda i: (0, 0, 0)),
                  pl.BlockSpec((db, d), lambda i: (blk, 0)),
                  pl.BlockSpec(memory_space=pl.ANY)],
        out_specs=pl.BlockSpec((db, d), lambda i: (blk, 0)),
        out_shape=jax.ShapeDtypeStruct(pooled.shape, pooled.dtype),
        input_output_aliases={2: 0},
        compiler_params=_params("arbitrary"),
        name="pool_sample",
    )(state, x, pooled)


def _route_rows(logits_t, bias_ref):
    s = 1.0 / (1.0 + jnp.exp(-logits_t))
    sr = [s[e:e + 1, :] for e in range(N_EXPERTS)]
    br = [sr[e] + bias_ref[e:e + 1, :] for e in range(N_EXPERTS)]
    gscore = []
    for g in range(N_EXPERT_GROUPS):
        r = br[g * EXPERTS_PER_GROUP:(g + 1) * EXPERTS_PER_GROUP]
        best2 = None
        for a in range(EXPERTS_PER_GROUP):
            for b in range(a + 1, EXPERTS_PER_GROUP):
                pair = r[a] + r[b]
                best2 = pair if best2 is None else jnp.maximum(best2, pair)
        gscore.append(best2)
    top = functools.reduce(jnp.maximum, gscore)
    best = jnp.full(top.shape, N_EXPERT_GROUPS - 1, jnp.int32)
    for g in range(N_EXPERT_GROUPS - 2, -1, -1):
        best = jnp.where(gscore[g] == top, g, best)
    sel = []
    for e in range(N_EXPERTS):
        g = e // EXPERTS_PER_GROUP
        ahead = jnp.zeros(top.shape, F32)
        for o in range(g * EXPERTS_PER_GROUP, (g + 1) * EXPERTS_PER_GROUP):
            if o == e:
                continue
            beats = (br[o] >= br[e]) if o < e else (br[o] > br[e])
            ahead = ahead + jnp.where(beats, 1.0, 0.0)
        sel.append(jnp.where(ahead < float(TOP_K), 1.0, 0.0) * jnp.where(best == g, 1.0, 0.0))
    den = functools.reduce(lambda a, b: a + b, [sel[e] * sr[e] for e in range(N_EXPERTS)])
    wts = [sel[e] * sr[e] / den for e in range(N_EXPERTS)]
    return sel, wts


def _post_mix_kernel(a_ref, x_ref, w_ref, sc_ref, g_ref, b_ref, wrh_ref, wrl_ref, rb_ref, tri_ref,
                     x1_ref, x1b_ref, idx_ref, wt_ref, rank_ref, cnt_ref, cnt_scr, *, alpha):
    i = pl.program_id(0)

    @pl.when(i == 0)
    def _():
        cnt_scr[...] = jnp.zeros(cnt_scr.shape, F32)

    mix = jnp.dot(a_ref[...], w_ref[...], preferred_element_type=F32) * sc_ref[...]
    x1 = _layer_norm(alpha * x_ref[...] + mix, g_ref[...], b_ref[...])
    x1_ref[...] = x1
    x_hi = x1.astype(BF16)
    x1b_ref[...] = x_hi
    x_lo = (x1 - x_hi.astype(F32)).astype(BF16)
    nt = (((1,), (1,)), ((), ()))
    wrh = wrh_ref[...]
    logits_t = (lax.dot_general(wrh, x_hi, nt, preferred_element_type=F32)
                + lax.dot_general(wrh, x_lo, nt, preferred_element_type=F32)
                + lax.dot_general(wrl_ref[...], x_hi, nt, preferred_element_type=F32))
    sel, wts = _route_rows(logits_t, rb_ref)
    erow = lax.broadcasted_iota(jnp.int32, logits_t.shape, 0)
    sel_t = jnp.zeros(logits_t.shape, F32)
    for e in range(N_EXPERTS):
        sel_t = jnp.where(erow == e, sel[e], sel_t)
    before = jnp.dot(sel_t.astype(BF16), tri_ref[...], preferred_element_type=F32)
    rank_t = cnt_scr[...] + before
    cnt_new = cnt_scr[...] + jnp.sum(sel_t, -1, keepdims=True)
    cnt_scr[...] = cnt_new
    cnt_ref[...] = cnt_new
    taken = jnp.zeros(sel[0].shape, F32)
    acc = [[jnp.zeros(sel[0].shape, F32) for _ in range(3)] for _ in range(TOP_K)]
    for e in range(N_EXPERTS):
        for k in range(TOP_K):
            hit = sel[e] * jnp.where(taken == float(k), 1.0, 0.0)
            acc[k][0] = acc[k][0] + hit * float(e)
            acc[k][1] = acc[k][1] + hit * wts[e]
            acc[k][2] = acc[k][2] + hit * rank_t[e:e + 1, :]
        taken = taken + sel[e]
    for k in range(TOP_K):
        idx_ref[k:k + 1, :] = acc[k][0].astype(jnp.int32)
        wt_ref[k:k + 1, :] = acc[k][1]
        rank_ref[k:k + 1, :] = acc[k][2].astype(jnp.int32)


def _post_mix(a, x, w, scale, g, b, wr_hi, wr_lo, rbias, tri, alpha, tm):
    n, d = x.shape
    full = lambda t: pl.BlockSpec(t.shape, lambda i: (0,) * t.ndim)
    rows = lambda wd: pl.BlockSpec((tm, wd), lambda i: (i, 0))
    cols = pl.BlockSpec((TOP_K, tm), lambda i: (0, i))
    return pl.pallas_call(
        functools.partial(_post_mix_kernel, alpha=alpha),
        grid=(n // tm,),
        in_specs=[rows(d), rows(d), full(w), full(scale), full(g), full(b), full(wr_hi), full(wr_lo),
                  full(rbias), full(tri)],
        out_specs=[rows(d), rows(d), cols, cols, cols, pl.BlockSpec((N_EXPERTS, 1), lambda i: (0, 0))],
        out_shape=[jax.ShapeDtypeStruct((n, d), F32), jax.ShapeDtypeStruct((n, d), BF16),
                   jax.ShapeDtypeStruct((TOP_K, n), jnp.int32), jax.ShapeDtypeStruct((TOP_K, n), F32),
                   jax.ShapeDtypeStruct((TOP_K, n), jnp.int32), jax.ShapeDtypeStruct((N_EXPERTS, 1), F32)],
        scratch_shapes=[pltpu.VMEM((N_EXPERTS, 1), F32)],
        compiler_params=_params("arbitrary"),
        name="post_mix",
    )(a, x, w, scale, g, b, wr_hi, wr_lo, rbias, tri)


def _moe_kernel(te_ref, nu_ref, x_ref, wr_ref, wg_ref, wu_ref, wd_ref, o_ref):
    del te_ref

    @pl.when(pl.program_id(0) < nu_ref[0])
    def _():
        x = x_ref[...]
        gate = jnp.dot(x, wg_ref[...], preferred_element_type=F32)
        up = jnp.dot(x, wu_ref[...], preferred_element_type=F32)
        h = gate * (1.0 / (1.0 + jnp.exp(-gate))) * up * wr_ref[...]
        o_ref[...] = jnp.dot(h.astype(BF16), wd_ref[...], preferred_element_type=F32)


def _moe(tile_expert, n_used, xs, wrow, w_gate, w_up, w_down, tile):
    p, d = xs.shape
    f = w_gate.shape[-1]
    row_blk = lambda i, te, nu: (jnp.minimum(i, nu[0] - 1), 0)
    grid_spec = pltpu.PrefetchScalarGridSpec(
        num_scalar_prefetch=2,
        grid=(p // tile,),
        in_specs=[pl.BlockSpec((tile, d), row_blk), pl.BlockSpec((tile, 1), row_blk),
                  pl.BlockSpec((None, d, f), lambda i, te, nu: (te[i], 0, 0)),
                  pl.BlockSpec((None, d, f), lambda i, te, nu: (te[i], 0, 0)),
                  pl.BlockSpec((None, f, d), lambda i, te, nu: (te[i], 0, 0))],
        out_specs=pl.BlockSpec((tile, d), row_blk),
    )
    return pl.pallas_call(
        _moe_kernel,
        grid_spec=grid_spec,
        out_shape=jax.ShapeDtypeStruct((p, d), F32),
        compiler_params=_params("arbitrary"),
        name="moe_grouped",
    )(tile_expert, n_used, xs, wrow, w_gate, w_up, w_down)


def _combine_kernel(x_ref, y_ref, g_ref, b_ref, o_ref, *, alpha):
    y = alpha * x_ref[...] + y_ref[0] + y_ref[1]
    o_ref[...] = _layer_norm(y, g_ref[...], b_ref[...])


def _combine(x1, yg, g, b, alpha, tm):
    n, d = x1.shape
    full = lambda t: pl.BlockSpec(t.shape, lambda i: (0,) * t.ndim)
    return pl.pallas_call(
        functools.partial(_combine_kernel, alpha=alpha),
        grid=(n // tm,),
        in_specs=[pl.BlockSpec((tm, d), lambda i: (i, 0)), pl.BlockSpec((TOP_K, tm, d), lambda i: (0, i, 0)),
                  full(g), full(b)],
        out_specs=pl.BlockSpec((tm, d), lambda i: (i, 0)),
        out_shape=jax.ShapeDtypeStruct((n, d), F32),
        compiler_params=_params("parallel"),
        name="combine_ln",
    )(x1, yg, g, b)


def _dispatch_tables(idx, rank, wts, counts, n_tiles):
    n = idx.shape[1]
    padded = ((counts + MOE_TILE - 1) // MOE_TILE) * MOE_TILE
    ends = jnp.cumsum(padded)
    pos = (ends - padded)[idx] + rank
    flat = pos.reshape(-1)
    tok = jnp.tile(jnp.arange(n, dtype=jnp.int32), TOP_K)
    rows = n_tiles * MOE_TILE
    src = jnp.zeros((rows,), jnp.int32).at[flat].set(tok, unique_indices=True)
    wrow = jnp.zeros((rows,), F32).at[flat].set(wts.reshape(-1), unique_indices=True)
    tile_start = jnp.arange(n_tiles, dtype=jnp.int32) * MOE_TILE
    tile_expert = jnp.minimum(jnp.searchsorted(ends, tile_start, side="right"), N_EXPERTS - 1).astype(jnp.int32)
    n_used = (ends[-1:] // MOE_TILE).astype(jnp.int32)
    return flat, src, wrow.reshape(rows, 1), tile_expert, n_used


def _rope_tables(pos):
    inv = ROPE_THETA ** (-2.0 * jnp.arange(HALF_ROPE, dtype=F32) / QK_ROPE)
    ang = pos.astype(F32)[:, None] * inv[None, :]
    cos, sin = jnp.cos(ang), jnp.sin(ang)
    n = pos.shape[0]
    ones = jnp.ones((n, ROPE_LANE0), F32)
    zeros = jnp.zeros((n, ROPE_LANE0), F32)
    tail1 = jnp.ones((n, LANES - ROPE_LANE0 - QK_ROPE), F32)
    tail0 = jnp.zeros((n, LANES - ROPE_LANE0 - QK_ROPE), F32)
    zh = jnp.zeros((n, HALF_ROPE), F32)
    cos_t = jnp.concatenate([ones, cos, cos, tail1], 1)
    sin_up = jnp.concatenate([zeros, zh, sin, tail0], 1)
    sin_dn = jnp.concatenate([zeros, -sin, zh, tail0], 1)
    return cos_t, sin_up, sin_dn


def _pad_heads(w, per_head):
    k = w.shape[0]
    w = w.reshape(k, N_HEADS, per_head)
    return jnp.pad(w, ((0, 0), (0, 0), (0, LANES - per_head))).reshape(k, N_HEADS * LANES)


def kernel(x_prompt, x_sample, cache_ckv, cache_krope, state_pool, page_table, w_dq, g_q, w_uq, w_dkv, g_kv,
           w_uk, w_uv, w_o, w_pool, pool_scale, w_router, router_bias, w_gate, w_up, w_down,
           ln1_g, ln1_b, ln2_g, ln2_b):
    batch, seq, d = x_prompt.shape
    db, dec_t, _ = x_sample.shape
    assert dec_t == 1 and d % LANES == 0 and seq % ATTN_TILE == 0 and seq % POOL_TILE == 0
    depth = ln1_g.shape[0]
    alpha = (2 * depth) ** 0.25
    n_prompt = batch * seq
    n = n_prompt + db
    assert n % db == 0 and db % 8 == 0
    tm = 384 if n % 384 == 0 else 128
    assert n % tm == 0
    n_pages = page_table.shape[1]
    past_len = n_pages * cache_ckv.shape[2]
    pps = min(PAGES_PER_STEP, n_pages)
    assert n_pages % pps == 0
    n_moe_tiles = (TOP_K * n) // MOE_TILE + N_EXPERTS
    row = lambda v: v.reshape(1, -1).astype(F32)

    pos = jnp.concatenate([jnp.tile(jnp.arange(seq), batch), jnp.full((db,), past_len)])
    cos_t, sin_up, sin_dn = _rope_tables(pos)
    tri = (jnp.arange(tm)[:, None] < jnp.arange(tm)[None, :]).astype(BF16)
    wr_t = w_router.T.astype(F32)
    wr_hi = wr_t.astype(BF16)
    wr_lo = (wr_t - wr_hi.astype(F32)).astype(BF16)
    rbias = router_bias.reshape(N_EXPERTS, 1).astype(F32)
    ones_d = jnp.ones((1, d), F32)

    x = jnp.concatenate([x_prompt.reshape(n_prompt, d), x_sample.reshape(db, d)], 0)
    ckv_p, kr_p, pool_p, ckv_s, kr_s, pool_s = [], [], [], [], [], []
    for i in range(depth):
        j = i // 2
        if i % 2 == 0:
            wuq_p = _pad_heads(w_uq[j], QK_NOPE + QK_ROPE).astype(BF16)
            wdkv_p = jnp.concatenate(
                [w_dkv[j][:, :KV_LORA], jnp.zeros((d, ROPE_LANE0), F32), w_dkv[j][:, KV_LORA:],
                 jnp.zeros((d, LANES - ROPE_LANE0 - QK_ROPE), F32)], 1).astype(BF16)
            wuk_p = _pad_heads(w_uk[j].reshape(KV_LORA, N_HEADS * QK_NOPE), QK_NOPE).astype(BF16)
            wuv_b = w_uv[j].reshape(KV_LORA, N_HEADS * V_HEAD).astype(BF16)
            q_all, k_all, v_all, ckv_all, kr_all = _mla_proj(
                x, w_dq[j].astype(BF16), row(g_q[j]), wuq_p, wdkv_p, row(g_kv[j]), wuk_p, wuv_b,
                cos_t, sin_up, sin_dn, tm)
            kr_all = kr_all[:, ROPE_LANE0:ROPE_LANE0 + QK_ROPE]
            o_all = _flash_attention(q_all, k_all, v_all, batch, seq, ATTN_TILE)
            w_ukt = jnp.pad(jnp.transpose(w_uk[j], (1, 2, 0)), ((0, 0), (0, LANES - QK_NOPE), (0, 0))).astype(BF16)
            q_lat = jnp.transpose(_q_latent(q_all, w_ukt, db), (1, 0, 2))
            q_rope = q_all[n_prompt:].reshape(db, N_HEADS, LANES)[:, :, ROPE_LANE0:ROPE_LANE0 + QK_ROPE]
            o_lat = _paged_attention(page_table, q_lat, q_rope, ckv_all[n_prompt:].reshape(db, 1, KV_LORA),
                                     kr_all[n_prompt:].reshape(db, 1, QK_ROPE), cache_ckv, cache_krope, j, pps)
            wv = jnp.transpose(w_uv[j], (1, 0, 2))
            wv_even = jnp.pad(wv, ((0, 0), (0, 0), (0, V_HEAD)))
            wv_odd = jnp.pad(wv, ((0, 0), (0, 0), (V_HEAD, 0)))
            wv_pairs = jnp.where((jnp.arange(N_HEADS) % 2 == 0)[:, None, None], wv_even, wv_odd).astype(BF16)
            mixed = _sample_out(jnp.transpose(o_lat, (1, 0, 2)), wv_pairs, o_all, db)
            w_mix, mix_scale = w_o[j].astype(BF16), ones_d
            ckv_p.append(ckv_all[:n_prompt].reshape(batch, seq, KV_LORA))
            kr_p.append(kr_all[:n_prompt].reshape(batch, seq, QK_ROPE))
            ckv_s.append(ckv_all[n_prompt:].reshape(db, 1, KV_LORA))
            kr_s.append(kr_all[n_prompt:].reshape(db, 1, QK_ROPE))
        else:
            pooled = _pool_prompt(x, batch, seq, POOL_TILE)
            mixed = _pool_sample(state_pool[j].astype(F32), x, pooled, db)
            w_mix = jax.scipy.linalg.block_diag(*[w_pool[j][g] for g in range(len(POOL_WINDOWS))]).astype(BF16)
            mix_scale = row(pool_scale[j])
            pool_p.append(x[:n_prompt].reshape(batch, seq, d)[:, seq - POOL_CTX:])
            pool_s.append(jnp.concatenate([state_pool[j][:, 1:].astype(F32), x[n_prompt:, None, :]], 1))
        x1, x1b, idx, wts, rank, counts = _post_mix(
            mixed, x, w_mix, mix_scale, row(ln1_g[i]), row(ln1_b[i]), wr_hi, wr_lo, rbias, tri, alpha, tm)
        flat_pos, src, wrow, tile_expert, n_used = _dispatch_tables(
            idx, rank, wts, counts.reshape(-1).astype(jnp.int32), n_moe_tiles)
        xs = jnp.take(x1b, src, axis=0, mode="clip")
        ys = _moe(tile_expert, n_used, xs, wrow, w_gate[i].astype(BF16), w_up[i].astype(BF16),
                  w_down[i].astype(BF16), MOE_TILE)
        yg = jnp.take(ys, flat_pos, axis=0, mode="clip").reshape(TOP_K, n, d)
        x = _combine(x1, yg, row(ln2_g[i]), row(ln2_b[i]), alpha, tm)
    return (x[:n_prompt].reshape(batch, seq, d), x[n_prompt:].reshape(db, 1, d),
            jnp.stack(ckv_p), jnp.stack(kr_p), jnp.stack(pool_p),
            jnp.stack(ckv_s), jnp.stack(kr_s), jnp.stack(pool_s))
```

```python
import functools
import math

import jax
import jax.numpy as jnp
from jax import lax
from jax.experimental import pallas as pl
from jax.experimental.pallas import tpu as pltpu

F32 = jnp.float32
BF16 = jnp.bfloat16

N_HEADS = 16
QK_NOPE = 64
QK_ROPE = 32
V_HEAD = 64
KV_LORA = 256
ROPE_THETA = 10000.0
ATTN_SCALE = (QK_NOPE + QK_ROPE) ** -0.5
POOL_WINDOWS = (2, 4, 8, 16)
POOL_CTX = max(POOL_WINDOWS) - 1
N_EXPERTS = 16
N_EXPERT_GROUPS = 4
EXPERTS_PER_GROUP = N_EXPERTS // N_EXPERT_GROUPS
TOP_K = 2
LN_EPS = 1e-5
RMS_EPS = 1e-6

PAIR_A = (0, 0, 0, 1, 1, 3)
PAIR_B = (1, 2, 3, 3, 2, 2)
N_PAIRS = len(PAIR_A)
N_CLASSES = N_EXPERT_GROUPS * N_PAIRS

LANES = 128
HALF_ROPE = QK_ROPE // 2
ROPE_LANE0 = QK_NOPE
HALO = 16
NEG_BIG = -1e30
LOG2E = math.log2(math.e)
VMEM_LIMIT = 52 * 1024 * 1024
MOE_TILE = 256
ATTN_TILE = 512
POOL_TILE = 512
PAGES_PER_STEP = 16
NT_DIMS = (((1,), (1,)), ((), ()))


def _params(*sem):
    return pltpu.CompilerParams(dimension_semantics=sem, vmem_limit_bytes=VMEM_LIMIT)


def _rms(x, g):
    return x * lax.rsqrt(jnp.mean(x * x, -1, keepdims=True) + RMS_EPS) * g


def _layer_norm(y, g, b):
    mu = jnp.mean(y, -1, keepdims=True)
    d = y - mu
    var = jnp.mean(d * d, -1, keepdims=True)
    return d * lax.rsqrt(var + LN_EPS) * g + b


def _rope_chunk(c, cos, sin_up, sin_dn):
    return c * cos + pltpu.roll(c, HALF_ROPE, 1) * sin_up + pltpu.roll(c, LANES - HALF_ROPE, 1) * sin_dn


def _mla_proj_kernel(x_ref, wdq_ref, gq_ref, wuq_ref, wdkv_ref, gkv_ref, wuk_ref, wuvt_ref,
                     cos_ref, sup_ref, sdn_ref, q_ref, k_ref, vt_ref, ckv_ref, kr_ref):
    xb = x_ref[...].astype(BF16)
    cos, sup, sdn = cos_ref[...], sup_ref[...], sdn_ref[...]
    cq = _rms(jnp.dot(xb, wdq_ref[...], preferred_element_type=F32), gq_ref[...])
    q = jnp.dot(cq.astype(BF16), wuq_ref[...], preferred_element_type=F32)
    kv = jnp.dot(xb, wdkv_ref[...], preferred_element_type=F32)
    ckv = _rms(kv[:, :KV_LORA], gkv_ref[...])
    ckv_ref[...] = ckv
    kr = _rope_chunk(kv[:, KV_LORA:], cos, sup, sdn)
    kr_ref[...] = kr
    ckv_b = ckv.astype(BF16)
    kn = jnp.dot(ckv_b, wuk_ref[...], preferred_element_type=F32)
    vt_ref[...] = lax.dot_general(wuvt_ref[...], ckv_b, NT_DIMS, preferred_element_type=F32).astype(BF16)
    for h in range(N_HEADS):
        sl = slice(h * LANES, (h + 1) * LANES)
        q_ref[:, sl] = (_rope_chunk(q[:, sl], cos, sup, sdn) * (ATTN_SCALE * LOG2E)).astype(BF16)
        k_ref[:, sl] = (kn[:, sl] + kr).astype(BF16)


def _mla_proj(x, wdq, gq, wuq, wdkv, gkv, wuk, wuvt, cos, sup, sdn, tm):
    n, d = x.shape
    full = lambda a: pl.BlockSpec(a.shape, lambda i: (0,) * a.ndim)
    rows = lambda w: pl.BlockSpec((tm, w), lambda i: (i, 0))
    hp = N_HEADS * LANES
    hv = N_HEADS * V_HEAD
    return pl.pallas_call(
        _mla_proj_kernel,
        grid=(n // tm,),
        in_specs=[rows(d), full(wdq), full(gq), full(wuq), full(wdkv), full(gkv), full(wuk), full(wuvt),
                  rows(LANES), rows(LANES), rows(LANES)],
        out_specs=[rows(hp), rows(hp), pl.BlockSpec((hv, tm), lambda i: (0, i)), rows(KV_LORA), rows(LANES)],
        out_shape=[jax.ShapeDtypeStruct((n, hp), BF16), jax.ShapeDtypeStruct((n, hp), BF16),
                   jax.ShapeDtypeStruct((hv, n), BF16),
                   jax.ShapeDtypeStruct((n, KV_LORA), F32), jax.ShapeDtypeStruct((n, LANES), F32)],
        compiler_params=_params("parallel"),
        name="mla_proj",
    )(x, wdq, gq, wuq, wdkv, gkv, wuk, wuvt, cos, sup, sdn)


def _flash_kernel(q_ref, k_ref, vt_ref, o_ref, *, tile):
    qi = pl.program_id(2)
    krow = lax.broadcasted_iota(jnp.int32, (tile, tile), 0)
    qcol = lax.broadcasted_iota(jnp.int32, (tile, tile), 1)

    def step(kv, carry, masked):
        start = pl.multiple_of(kv * tile, tile)
        new = []
        for hh in range(2):
            m, l, acc = carry[hh]
            hs = slice(hh * LANES, (hh + 1) * LANES)
            s = lax.dot_general(k_ref[pl.ds(start, tile), hs], q_ref[:, hs], NT_DIMS, preferred_element_type=F32)
            if masked:
                s = jnp.where(krow <= qcol, s, NEG_BIG)
            m_new = jnp.maximum(m, jnp.max(s, 0, keepdims=True))
            p = jnp.exp2(s - m_new)
            a = jnp.exp2(m - m_new)
            l = a * l + jnp.sum(p, 0, keepdims=True)
            vt = vt_ref[hh * V_HEAD:(hh + 1) * V_HEAD, pl.ds(start, tile)]
            acc = a * acc + jnp.dot(vt, p.astype(BF16), preferred_element_type=F32)
            new.append((m_new, l, acc))
        return tuple(new)

    init = tuple((jnp.full((1, tile), NEG_BIG, F32), jnp.zeros((1, tile), F32), jnp.zeros((V_HEAD, tile), F32))
                 for _ in range(2))
    carry = lax.fori_loop(0, qi, lambda kv, c: step(kv, c, False), init)
    carry = step(qi, carry, True)
    o_t = jnp.concatenate([acc / l for _, l, acc in carry], axis=0)
    o_ref[...] = o_t.T.astype(BF16)


def _flash_attention(q, k, vt, batch, seq, tile):
    n = q.shape[0]
    nq = seq // tile
    pairs = N_HEADS // 2
    return pl.pallas_call(
        functools.partial(_flash_kernel, tile=tile),
        grid=(batch, pairs, nq),
        in_specs=[pl.BlockSpec((tile, 2 * LANES), lambda b, p, i: (b * nq + i, p)),
                  pl.BlockSpec((seq, 2 * LANES), lambda b, p, i: (b, p)),
                  pl.BlockSpec((2 * V_HEAD, seq), lambda b, p, i: (p, b))],
        out_specs=pl.BlockSpec((tile, 2 * V_HEAD), lambda b, p, i: (b * nq + i, p)),
        out_shape=jax.ShapeDtypeStruct((n, N_HEADS * V_HEAD), BF16),
        compiler_params=_params("parallel", "parallel", "arbitrary"),
        name="flash_prompt",
    )(q, k, vt)


def _qlat_kernel(q_ref, w_ref, o_ref):
    o_ref[...] = jnp.dot(q_ref[...], w_ref[...], preferred_element_type=F32).astype(BF16)


def _q_latent(q, w_ukt, db):
    n = q.shape[0]
    blk = n // db - 1
    return pl.pallas_call(
        _qlat_kernel,
        grid=(N_HEADS,),
        in_specs=[pl.BlockSpec((db, LANES), lambda h: (blk, h)),
                  pl.BlockSpec((None, LANES, KV_LORA), lambda h: (h, 0, 0))],
        out_specs=pl.BlockSpec((None, db, KV_LORA), lambda h: (h, 0, 0)),
        out_shape=jax.ShapeDtypeStruct((N_HEADS, db, KV_LORA), BF16),
        compiler_params=_params("parallel"),
        name="q_latent",
    )(q, w_ukt)


def _paged_kernel(pt_ref, ql_ref, qr_ref, cnew_ref, knew_ref, *rest, pps, psz):
    ckv_refs = rest[:pps]
    krt_refs = rest[pps:2 * pps]
    o_ref = rest[2 * pps]
    kbuf, rbuf, m_scr, l_scr, acc_scr = rest[2 * pps + 1:]
    c = pl.program_id(1)

    @pl.when(c == 0)
    def _():
        m_scr[...] = jnp.full(m_scr.shape, NEG_BIG, F32)
        l_scr[...] = jnp.zeros(l_scr.shape, F32)
        acc_scr[...] = jnp.zeros(acc_scr.shape, F32)

    for i in range(pps):
        kbuf[i * psz:(i + 1) * psz, :] = ckv_refs[i][...].astype(BF16)
        rbuf[:, i * psz:(i + 1) * psz] = krt_refs[i][...].astype(BF16)
    ql = ql_ref[...]
    qr = qr_ref[...]
    keys = kbuf[...]
    s = (lax.dot_general(ql, keys, NT_DIMS, preferred_element_type=F32)
         + jnp.dot(qr, rbuf[...], preferred_element_type=F32))
    m = m_scr[...]
    m_new = jnp.maximum(m, jnp.max(s, -1, keepdims=True))
    p = jnp.exp2(s - m_new).astype(BF16)
    a = jnp.exp2(m - m_new)
    l_new = a * l_scr[...] + jnp.sum(p.astype(F32), -1, keepdims=True)
    acc_new = a * acc_scr[...] + jnp.dot(p, keys, preferred_element_type=F32)
    m_scr[...] = m_new
    l_scr[...] = l_new
    acc_scr[...] = acc_new

    @pl.when(c == pl.num_programs(1) - 1)
    def _():
        cn = cnew_ref[...].astype(BF16).astype(F32)
        kn = knew_ref[...].astype(BF16).astype(F32)
        s_self = (jnp.sum(ql.astype(F32) * cn, -1, keepdims=True)
                  + jnp.sum(qr.astype(F32) * kn, -1, keepdims=True))
        m_f = jnp.maximum(m_new, s_self)
        a_f = jnp.exp2(m_new - m_f)
        p_self = jnp.exp2(s_self - m_f).astype(BF16).astype(F32)
        o_ref[...] = (a_f * acc_new + p_self * cn) / (a_f * l_new + p_self)


def _paged_attention(page_table, q_lat, q_rope, c_new, k_new, cache_ckv, cache_krt, layer, pps):
    db, n_pages = page_table.shape
    psz = cache_ckv.shape[2]
    n_chunks = n_pages // pps
    pt = page_table.reshape(-1)

    def page_spec(rows, width, i):
        return pl.BlockSpec((None, None, rows, width),
                            lambda b, c, pt_ref: (layer, pt_ref[b * n_pages + c * pps + i], 0, 0))

    per_b = lambda r, w: pl.BlockSpec((None, r, w), lambda b, c, pt_ref: (b, 0, 0))
    grid_spec = pltpu.PrefetchScalarGridSpec(
        num_scalar_prefetch=1,
        grid=(db, n_chunks),
        in_specs=[per_b(N_HEADS, KV_LORA), per_b(N_HEADS, QK_ROPE), per_b(1, KV_LORA), per_b(1, QK_ROPE)]
                 + [page_spec(psz, KV_LORA, i) for i in range(pps)]
                 + [page_spec(QK_ROPE, psz, i) for i in range(pps)],
        out_specs=per_b(N_HEADS, KV_LORA),
        scratch_shapes=[pltpu.VMEM((pps * psz, KV_LORA), BF16), pltpu.VMEM((QK_ROPE, pps * psz), BF16),
                        pltpu.VMEM((N_HEADS, 1), F32), pltpu.VMEM((N_HEADS, 1), F32),
                        pltpu.VMEM((N_HEADS, KV_LORA), F32)],
    )
    return pl.pallas_call(
        functools.partial(_paged_kernel, pps=pps, psz=psz),
        grid_spec=grid_spec,
        out_shape=jax.ShapeDtypeStruct((db, N_HEADS, KV_LORA), F32),
        compiler_params=_params("parallel", "arbitrary"),
        name="paged_sample",
    )(pt, q_lat, q_rope, c_new, k_new, *([cache_ckv] * pps), *([cache_krt] * pps))


def _sample_out_kernel(ol_ref, w_ref, oin_ref, o_ref):
    del oin_ref
    acc = jnp.dot(ol_ref[0].astype(BF16), w_ref[0], preferred_element_type=F32)
    acc = acc + jnp.dot(ol_ref[1].astype(BF16), w_ref[1], preferred_element_type=F32)
    o_ref[...] = acc.astype(BF16)


def _sample_out(o_lat_t, w_uv_pairs, o_all, db):
    n = o_all.shape[0]
    blk = n // db - 1
    return pl.pallas_call(
        _sample_out_kernel,
        grid=(N_HEADS // 2,),
        in_specs=[pl.BlockSpec((2, db, KV_LORA), lambda p: (p, 0, 0)),
                  pl.BlockSpec((2, KV_LORA, 2 * V_HEAD), lambda p: (p, 0, 0)),
                  pl.BlockSpec(memory_space=pl.ANY)],
        out_specs=pl.BlockSpec((db, 2 * V_HEAD), lambda p: (blk, p)),
        out_shape=jax.ShapeDtypeStruct(o_all.shape, o_all.dtype),
        input_output_aliases={2: 0},
        compiler_params=_params("parallel"),
        name="sample_out",
    )(o_lat_t, w_uv_pairs, o_all)


def _pool_prompt_kernel(x_ref, halo_ref, o_ref, l0, l1, l2, l3, *, tile, seq):
    i = pl.program_id(0)
    d = x_ref.shape[1]
    gw = d // len(POOL_WINDOWS)
    ext = tile + HALO
    s0 = (i * tile) % seq
    pos_ext = s0 - HALO + lax.broadcasted_iota(jnp.int32, (ext, 1), 0)
    zeros = jnp.zeros((HALO, d), F32)
    for buf in (l0, l1, l2, l3):
        buf[0:HALO, :] = zeros
    l0[HALO:2 * HALO, :] = halo_ref[...]
    l0[2 * HALO:, :] = x_ref[...]

    def level(src, dst, k, c0):
        cur = src[HALO:HALO + ext, c0:]
        sh = src[HALO - k:HALO - k + ext, c0:]
        dst[HALO:HALO + ext, c0:] = cur + jnp.where(pos_ext >= k, sh, 0.0)

    level(l0, l1, 1, 0)
    level(l1, l2, 2, gw)
    level(l2, l3, 4, 2 * gw)
    pos = pos_ext[HALO:]
    x = x_ref[...]
    t0 = 2 * HALO
    sums = (l1[t0:, 0:gw], l2[t0:, gw:2 * gw], l3[t0:, 2 * gw:3 * gw],
            l3[t0:, 3 * gw:] + jnp.where(pos >= 8, l3[t0 - 8:t0 - 8 + tile, 3 * gw:], 0.0))
    for g, w in enumerate(POOL_WINDOWS):
        cnt = jnp.minimum(pos + 1, w).astype(F32)
        o_ref[:, g * gw:(g + 1) * gw] = (sums[g] / cnt - x[:, g * gw:(g + 1) * gw]).astype(BF16)


def _pool_prompt(x, batch, seq, tile):
    n, d = x.shape
    hb = tile // HALO
    return pl.pallas_call(
        functools.partial(_pool_prompt_kernel, tile=tile, seq=seq),
        grid=(batch * seq // tile,),
        in_specs=[pl.BlockSpec((tile, d), lambda i: (i, 0)),
                  pl.BlockSpec((HALO, d), lambda i: (jnp.maximum(i * hb - 1, 0), 0))],
        out_specs=pl.BlockSpec((tile, d), lambda i: (i, 0)),
        out_shape=jax.ShapeDtypeStruct((n, d), BF16),
        scratch_shapes=[pltpu.VMEM((tile + 2 * HALO, d), F32)] * 4,
        compiler_params=_params("parallel"),
        name="pool_prompt",
    )(x, x)


def _pool_sample_kernel(st_ref, x_ref, pin_ref, o_ref):
    del pin_ref
    d = x_ref.shape[1]
    gw = d // len(POOL_WINDOWS)
    x = x_ref[...]
    for g, w in enumerate(POOL_WINDOWS):
        cs = slice(g * gw, (g + 1) * gw)
        acc = x[:, cs]
        for r in range(POOL_CTX - (w - 1), POOL_CTX):
            acc = acc + st_ref[r, :, cs]
        o_ref[:, cs] = (acc / float(w) - x[:, cs]).astype(BF16)


def _pool_sample(state_t, x, pooled, db):
    n, d = x.shape
    blk = n // db - 1
    return pl.pallas_call(
        _pool_sample_kernel,
        grid=(1,),
        in_specs=[pl.BlockSpec(state_t.shape, lambda i: (0, 0, 0)),
                  pl.BlockSpec((db, d), lambda i: (blk, 0)),
                  pl.BlockSpec(memory_space=pl.ANY)],
        out_specs=pl.BlockSpec((db, d), lambda i: (blk, 0)),
        out_shape=jax.ShapeDtypeStruct(pooled.shape, pooled.dtype),
        input_output_aliases={2: 0},
        compiler_params=_params("arbitrary"),
        name="pool_sample",
    )(state_t, x, pooled)


def _route_rows(logits_t, bias_ref):
    s = 1.0 / (1.0 + jnp.exp(-logits_t))
    sr = [s[e:e + 1, :] for e in range(N_EXPERTS)]
    br = [sr[e] + bias_ref[e:e + 1, :] for e in range(N_EXPERTS)]
    gscore = []
    for g in range(N_EXPERT_GROUPS):
        r = br[g * EXPERTS_PER_GROUP:(g + 1) * EXPERTS_PER_GROUP]
        best2 = None
        for a in range(EXPERTS_PER_GROUP):
            for b in range(a + 1, EXPERTS_PER_GROUP):
                pair = r[a] + r[b]
                best2 = pair if best2 is None else jnp.maximum(best2, pair)
        gscore.append(best2)
    top = functools.reduce(jnp.maximum, gscore)
    best = jnp.full(top.shape, N_EXPERT_GROUPS - 1, jnp.int32)
    for g in range(N_EXPERT_GROUPS - 2, -1, -1):
        best = jnp.where(gscore[g] == top, g, best)
    sel = []
    for e in range(N_EXPERTS):
        g = e // EXPERTS_PER_GROUP
        ahead = jnp.zeros(top.shape, F32)
        for o in range(g * EXPERTS_PER_GROUP, (g + 1) * EXPERTS_PER_GROUP):
            if o == e:
                continue
            beats = (br[o] >= br[e]) if o < e else (br[o] > br[e])
            ahead = ahead + jnp.where(beats, 1.0, 0.0)
        sel.append(jnp.where(ahead < float(TOP_K), 1.0, 0.0) * jnp.where(best == g, 1.0, 0.0))
    den = functools.reduce(lambda a, b: a + b, [sel[e] * sr[e] for e in range(N_EXPERTS)])
    wts = [sel[e] * sr[e] / den for e in range(N_EXPERTS)]
    return sel, wts


def _post_mix_kernel(a_ref, x_ref, w_ref, sc_ref, g_ref, b_ref, wrh_ref, wrl_ref, rb_ref, tri_ref,
                     xe_ref, cls_ref, rank_ref, cnt_ref, cnt_scr, *, alpha):
    i = pl.program_id(0)
    d = x_ref.shape[1]

    @pl.when(i == 0)
    def _():
        cnt_scr[...] = jnp.zeros(cnt_scr.shape, F32)

    mix = jnp.dot(a_ref[...], w_ref[...], preferred_element_type=F32) * sc_ref[...]
    x1 = _layer_norm(alpha * x_ref[...] + mix, g_ref[...], b_ref[...])
    xe_ref[:, :d] = x1
    x_hi = x1.astype(BF16)
    x_lo = (x1 - x_hi.astype(F32)).astype(BF16)
    wrh = wrh_ref[...]
    logits_t = (lax.dot_general(wrh, x_hi, NT_DIMS, preferred_element_type=F32)
                + lax.dot_general(wrh, x_lo, NT_DIMS, preferred_element_type=F32)
                + lax.dot_general(wrl_ref[...], x_hi, NT_DIMS, preferred_element_type=F32))
    sel, wts = _route_rows(logits_t, rb_ref)
    t = logits_t.shape[1]
    ind = []
    w_a = jnp.zeros(sel[0].shape, F32)
    w_b = jnp.zeros(sel[0].shape, F32)
    cls = jnp.zeros(sel[0].shape, F32)
    for g in range(N_EXPERT_GROUPS):
        for k in range(N_PAIRS):
            ea = g * EXPERTS_PER_GROUP + PAIR_A[k]
            eb = g * EXPERTS_PER_GROUP + PAIR_B[k]
            hit = sel[ea] * sel[eb]
            w_a = w_a + hit * wts[ea]
            w_b = w_b + hit * wts[eb]
            cls = cls + hit * float(len(ind))
            ind.append(hit)
    crow = lax.broadcasted_iota(jnp.int32, (N_CLASSES, t), 0)
    ind_t = jnp.zeros((N_CLASSES, t), F32)
    for c in range(N_CLASSES):
        ind_t = jnp.where(crow == c, ind[c], ind_t)
    before = jnp.dot(ind_t.astype(BF16), tri_ref[...], preferred_element_type=F32)
    rank_t = cnt_scr[...] + before
    cnt_new = cnt_scr[...] + jnp.sum(ind_t, -1, keepdims=True)
    cnt_scr[...] = cnt_new
    cnt_ref[...] = cnt_new
    rank = jnp.zeros(sel[0].shape, F32)
    for c in range(N_CLASSES):
        rank = rank + ind[c] * rank_t[c:c + 1, :]
    cls_ref[...] = cls.astype(jnp.int32)
    rank_ref[...] = rank.astype(jnp.int32)
    lrow = lax.broadcasted_iota(jnp.int32, (LANES, t), 0)
    w_t = jnp.where(lrow == 0, w_a, jnp.where(lrow == 1, w_b, 0.0))
    xe_ref[:, d:] = w_t.T


def _post_mix(a, x, w, scale, g, b, wr_hi, wr_lo, rbias, tri, alpha, tm):
    n, d = x.shape
    full = lambda t: pl.BlockSpec(t.shape, lambda i: (0,) * t.ndim)
    rows = lambda wd: pl.BlockSpec((tm, wd), lambda i: (i, 0))
    cols = pl.BlockSpec((1, tm), lambda i: (0, i))
    return pl.pallas_call(
        functools.partial(_post_mix_kernel, alpha=alpha),
        grid=(n // tm,),
        in_specs=[rows(d), rows(d), full(w), full(scale), full(g), full(b), full(wr_hi), full(wr_lo),
                  full(rbias), full(tri)],
        out_specs=[rows(d + LANES), cols, cols, pl.BlockSpec((N_CLASSES, 1), lambda i: (0, 0))],
        out_shape=[jax.ShapeDtypeStruct((n, d + LANES), F32),
                   jax.ShapeDtypeStruct((1, n), jnp.int32), jax.ShapeDtypeStruct((1, n), jnp.int32),
                   jax.ShapeDtypeStruct((N_CLASSES, 1), F32)],
        scratch_shapes=[pltpu.VMEM((N_CLASSES, 1), F32)],
        compiler_params=_params("arbitrary"),
        name="post_mix",
    )(a, x, w, scale, g, b, wr_hi, wr_lo, rbias, tri)


def _row_copy(src_ref, src_row, dst_ref, dst_row, sem):
    return pltpu.make_async_copy(src_ref.at[pl.ds(src_row, 1), :], dst_ref.at[pl.ds(dst_row, 1), :], sem)


def _dispatch_kernel(pos_ref, x_ref, xs_in_ref, xs_ref, sem, *, tm):
    del xs_in_ref
    base = pl.program_id(0) * tm

    def start(r, carry):
        _row_copy(x_ref, r, xs_ref, pos_ref[base + r], sem).start()
        return carry

    def wait(r, carry):
        _row_copy(x_ref, r, xs_ref, pos_ref[base + r], sem).wait()
        return carry

    lax.fori_loop(0, tm, start, 0)
    lax.fori_loop(0, tm, wait, 0)


def _dispatch(pos, xe, xs_init, tm):
    n, w = xe.shape
    grid_spec = pltpu.PrefetchScalarGridSpec(
        num_scalar_prefetch=1,
        grid=(n // tm,),
        in_specs=[pl.BlockSpec((tm, w), lambda i, p: (i, 0)), pl.BlockSpec(memory_space=pl.ANY)],
        out_specs=pl.BlockSpec(memory_space=pl.ANY),
        scratch_shapes=[pltpu.SemaphoreType.DMA(())],
    )
    return pl.pallas_call(
        functools.partial(_dispatch_kernel, tm=tm),
        grid_spec=grid_spec,
        out_shape=jax.ShapeDtypeStruct(xs_init.shape, xs_init.dtype),
        input_output_aliases={2: 0},
        compiler_params=_params("arbitrary"),
        name="moe_dispatch",
    )(pos, xe, xs_init)


def _unpermute_kernel(pos_ref, xs_ref, o_ref, sem, *, tm):
    base = pl.program_id(0) * tm

    def start(r, carry):
        _row_copy(xs_ref, pos_ref[base + r], o_ref, r, sem).start()
        return carry

    def wait(r, carry):
        _row_copy(xs_ref, pos_ref[base + r], o_ref, r, sem).wait()
        return carry

    lax.fori_loop(0, tm, start, 0)
    lax.fori_loop(0, tm, wait, 0)


def _unpermute(pos, xs, n, tm):
    d = xs.shape[1]
    grid_spec = pltpu.PrefetchScalarGridSpec(
        num_scalar_prefetch=1,
        grid=(n // tm,),
        in_specs=[pl.BlockSpec(memory_space=pl.ANY)],
        out_specs=pl.BlockSpec((tm, d), lambda i, p: (i, 0)),
        scratch_shapes=[pltpu.SemaphoreType.DMA(())],
    )
    return pl.pallas_call(
        functools.partial(_unpermute_kernel, tm=tm),
        grid_spec=grid_spec,
        out_shape=jax.ShapeDtypeStruct((n, d), xs.dtype),
        compiler_params=_params("arbitrary"),
        name="moe_unpermute",
    )(pos, xs)


def _moe_kernel(ta_ref, tb_ref, nu_ref, x_ref, wga_ref, wua_ref, wda_ref, wgb_ref, wub_ref, wdb_ref,
                g_ref, b_ref, o_ref, wga_s, wua_s, wda_s, wgb_s, wub_s, wdb_s, *, alpha):
    i = pl.program_id(0)
    d = o_ref.shape[1]

    @pl.when(i < nu_ref[0])
    def _():
        prev = jnp.maximum(i - 1, 0)

        @pl.when((i == 0) | (ta_ref[i] != ta_ref[prev]))
        def _():
            wga_s[...] = wga_ref[...].astype(BF16)
            wua_s[...] = wua_ref[...].astype(BF16)
            wda_s[...] = wda_ref[...].astype(BF16)

        @pl.when((i == 0) | (tb_ref[i] != tb_ref[prev]))
        def _():
            wgb_s[...] = wgb_ref[...].astype(BF16)
            wub_s[...] = wub_ref[...].astype(BF16)
            wdb_s[...] = wdb_ref[...].astype(BF16)

        x = x_ref[:, :d]
        xb = x.astype(BF16)
        y = None
        for col, (wg, wu, wd) in enumerate(((wga_s, wua_s, wda_s), (wgb_s, wub_s, wdb_s))):
            gate = jnp.dot(xb, wg[...], preferred_element_type=F32)
            up = jnp.dot(xb, wu[...], preferred_element_type=F32)
            h = gate * (1.0 / (1.0 + jnp.exp(-gate))) * up * x_ref[:, d + col:d + col + 1]
            part = jnp.dot(h.astype(BF16), wd[...], preferred_element_type=F32)
            y = part if y is None else y + part
        o_ref[...] = _layer_norm(alpha * x + y, g_ref[...], b_ref[...])


def _moe(tile_a, tile_b, n_used, xs, w_gate, w_up, w_down, layer, g, b, alpha, tile):
    p, w = xs.shape
    d = w - LANES
    f = w_gate.shape[-1]
    row_blk = lambda i, ta, tb, nu: (jnp.minimum(i, nu[0] - 1), 0)
    wa = lambda r, c: pl.BlockSpec((None, None, r, c), lambda i, ta, tb, nu: (layer, ta[i], 0, 0))
    wb = lambda r, c: pl.BlockSpec((None, None, r, c), lambda i, ta, tb, nu: (layer, tb[i], 0, 0))
    full = lambda t: pl.BlockSpec(t.shape, lambda i, ta, tb, nu: (0,) * t.ndim)
    grid_spec = pltpu.PrefetchScalarGridSpec(
        num_scalar_prefetch=3,
        grid=(p // tile,),
        in_specs=[pl.BlockSpec((tile, w), row_blk), wa(d, f), wa(d, f), wa(f, d), wb(d, f), wb(d, f), wb(f, d),
                  full(g), full(b)],
        out_specs=pl.BlockSpec((tile, d), row_blk),
        scratch_shapes=[pltpu.VMEM((d, f), BF16), pltpu.VMEM((d, f), BF16), pltpu.VMEM((f, d), BF16)] * 2,
    )
    return pl.pallas_call(
        functools.partial(_moe_kernel, alpha=alpha),
        grid_spec=grid_spec,
        out_shape=jax.ShapeDtypeStruct((p, d), F32),
        compiler_params=_params("arbitrary"),
        name="moe_grouped",
    )(tile_a, tile_b, n_used, xs, w_gate, w_up, w_down, w_gate, w_up, w_down, g, b)


def _dispatch_tables(cls, rank, counts, n_tiles):
    padded = ((counts + MOE_TILE - 1) // MOE_TILE) * MOE_TILE
    ends = jnp.cumsum(padded)
    pos = (ends - padded)[cls] + rank
    tile_start = jnp.arange(n_tiles, dtype=jnp.int32) * MOE_TILE
    tile_cls = jnp.minimum(jnp.sum((tile_start[:, None] >= ends[None, :]).astype(jnp.int32), 1), N_CLASSES - 1)
    group0 = (tile_cls // N_PAIRS) * EXPERTS_PER_GROUP
    tile_a = group0 + jnp.asarray(PAIR_A, jnp.int32)[tile_cls % N_PAIRS]
    tile_b = group0 + jnp.asarray(PAIR_B, jnp.int32)[tile_cls % N_PAIRS]
    n_used = (ends[-1:] // MOE_TILE).astype(jnp.int32)
    return pos.astype(jnp.int32), tile_a.astype(jnp.int32), tile_b.astype(jnp.int32), n_used


def _rope_tables(pos):
    inv = ROPE_THETA ** (-2.0 * jnp.arange(HALF_ROPE, dtype=F32) / QK_ROPE)
    ang = pos.astype(F32)[:, None] * inv[None, :]
    cos, sin = jnp.cos(ang), jnp.sin(ang)
    n = pos.shape[0]
    ones = jnp.ones((n, ROPE_LANE0), F32)
    zeros = jnp.zeros((n, ROPE_LANE0), F32)
    tail1 = jnp.ones((n, LANES - ROPE_LANE0 - QK_ROPE), F32)
    tail0 = jnp.zeros((n, LANES - ROPE_LANE0 - QK_ROPE), F32)
    zh = jnp.zeros((n, HALF_ROPE), F32)
    cos_t = jnp.concatenate([ones, cos, cos, tail1], 1)
    sin_up = jnp.concatenate([zeros, zh, sin, tail0], 1)
    sin_dn = jnp.concatenate([zeros, -sin, zh, tail0], 1)
    return cos_t, sin_up, sin_dn


def _pad_heads(w, per_head):
    k = w.shape[0]
    w = w.reshape(k, N_HEADS, per_head)
    return jnp.pad(w, ((0, 0), (0, 0), (0, LANES - per_head))).reshape(k, N_HEADS * LANES)


def kernel(x_prompt, x_sample, cache_ckv, cache_krope, state_pool, page_table, w_dq, g_q, w_uq, w_dkv, g_kv,
           w_uk, w_uv, w_o, w_pool, pool_scale, w_router, router_bias, w_gate, w_up, w_down,
           ln1_g, ln1_b, ln2_g, ln2_b):
    batch, seq, d = x_prompt.shape
    db, dec_t, _ = x_sample.shape
    assert dec_t == 1 and d % LANES == 0 and seq % ATTN_TILE == 0 and seq % POOL_TILE == 0
    depth = ln1_g.shape[0]
    alpha = (2 * depth) ** 0.25
    n_prompt = batch * seq
    n = n_prompt + db
    assert n % db == 0 and db % 8 == 0
    tm = 384 if n % 384 == 0 else 128
    assert n % tm == 0
    n_pages = page_table.shape[1]
    past_len = n_pages * cache_ckv.shape[2]
    pps = min(PAGES_PER_STEP, n_pages)
    assert n_pages % pps == 0
    n_moe_tiles = (n + N_CLASSES * (MOE_TILE - 1)) // MOE_TILE
    row = lambda v: v.reshape(1, -1).astype(F32)

    pos = jnp.concatenate([jnp.tile(jnp.arange(seq), batch), jnp.full((db,), past_len)])
    cos_t, sin_up, sin_dn = _rope_tables(pos)
    tri = (jnp.arange(tm)[:, None] < jnp.arange(tm)[None, :]).astype(BF16)
    wr_t = w_router.T.astype(F32)
    wr_hi = wr_t.astype(BF16)
    wr_lo = (wr_t - wr_hi.astype(F32)).astype(BF16)
    rbias = router_bias.reshape(N_EXPERTS, 1).astype(F32)
    ones_d = jnp.ones((1, d), F32)
    cache_krt = jnp.swapaxes(cache_krope, 2, 3)
    xs_init = jnp.zeros((n_moe_tiles * MOE_TILE, d + LANES), F32)

    x = jnp.concatenate([x_prompt.reshape(n_prompt, d), x_sample.reshape(db, d)], 0)
    ckv_p, kr_p, pool_p, ckv_s, kr_s, pool_s = [], [], [], [], [], []
    for i in range(depth):
        j = i // 2
        if i % 2 == 0:
            wuq_p = _pad_heads(w_uq[j], QK_NOPE + QK_ROPE).astype(BF16)
            wdkv_p = jnp.concatenate(
                [w_dkv[j][:, :KV_LORA], jnp.zeros((d, ROPE_LANE0), F32), w_dkv[j][:, KV_LORA:],
                 jnp.zeros((d, LANES - ROPE_LANE0 - QK_ROPE), F32)], 1).astype(BF16)
            wuk_p = _pad_heads(w_uk[j].reshape(KV_LORA, N_HEADS * QK_NOPE), QK_NOPE).astype(BF16)
            wuv_t = w_uv[j].reshape(KV_LORA, N_HEADS * V_HEAD).T.astype(BF16)
            q_all, k_all, vt_all, ckv_all, kr_all = _mla_proj(
                x, w_dq[j].astype(BF16), row(g_q[j]), wuq_p, wdkv_p, row(g_kv[j]), wuk_p, wuv_t,
                cos_t, sin_up, sin_dn, tm)
            kr_all = kr_all[:, ROPE_LANE0:ROPE_LANE0 + QK_ROPE]
            o_all = _flash_attention(q_all, k_all, vt_all, batch, seq, ATTN_TILE)
            w_ukt = jnp.pad(jnp.transpose(w_uk[j], (1, 2, 0)), ((0, 0), (0, LANES - QK_NOPE), (0, 0))).astype(BF16)
            q_lat = jnp.transpose(_q_latent(q_all, w_ukt, db), (1, 0, 2))
            q_rope = q_all[n_prompt:].reshape(db, N_HEADS, LANES)[:, :, ROPE_LANE0:ROPE_LANE0 + QK_ROPE]
            o_lat = _paged_attention(page_table, q_lat, q_rope, ckv_all[n_prompt:].reshape(db, 1, KV_LORA),
                                     kr_all[n_prompt:].reshape(db, 1, QK_ROPE), cache_ckv, cache_krt, j, pps)
            wv = jnp.transpose(w_uv[j], (1, 0, 2))
            wv_even = jnp.pad(wv, ((0, 0), (0, 0), (0, V_HEAD)))
            wv_odd = jnp.pad(wv, ((0, 0), (0, 0), (V_HEAD, 0)))
            wv_pairs = jnp.where((jnp.arange(N_HEADS) % 2 == 0)[:, None, None], wv_even, wv_odd).astype(BF16)
            mixed = _sample_out(jnp.transpose(o_lat, (1, 0, 2)), wv_pairs, o_all, db)
            w_mix, mix_scale = w_o[j].astype(BF16), ones_d
            ckv_p.append(ckv_all[:n_prompt].reshape(batch, seq, KV_LORA))
            kr_p.append(kr_all[:n_prompt].reshape(batch, seq, QK_ROPE))
            ckv_s.append(ckv_all[n_prompt:].reshape(db, 1, KV_LORA))
            kr_s.append(kr_all[n_prompt:].reshape(db, 1, QK_ROPE))
        else:
            pooled = _pool_prompt(x, batch, seq, POOL_TILE)
            mixed = _pool_sample(jnp.transpose(state_pool[j], (1, 0, 2)).astype(F32), x, pooled, db)
            w_mix = jax.scipy.linalg.block_diag(*[w_pool[j][g] for g in range(len(POOL_WINDOWS))]).astype(BF16)
            mix_scale = row(pool_scale[j])
            pool_p.append(x[:n_prompt].reshape(batch, seq, d)[:, seq - POOL_CTX:])
            pool_s.append(jnp.concatenate([state_pool[j][:, 1:].astype(F32), x[n_prompt:, None, :]], 1))
        xe, cls, rank, counts = _post_mix(
            mixed, x, w_mix, mix_scale, row(ln1_g[i]), row(ln1_b[i]), wr_hi, wr_lo, rbias, tri, alpha, tm)
        row_pos, tile_a, tile_b, n_used = _dispatch_tables(
            cls.reshape(-1), rank.reshape(-1), counts.reshape(-1).astype(jnp.int32), n_moe_tiles)
        xs = _dispatch(row_pos, xe, xs_init, tm)
        ys = _moe(tile_a, tile_b, n_used, xs, w_gate, w_up, w_down, i,
                  row(ln2_g[i]), row(ln2_b[i]), alpha, MOE_TILE)
        x = _unpermute(row_pos, ys, n, tm)
    return (x[:n_prompt].reshape(batch, seq, d), x[n_prompt:].reshape(db, 1, d),
            jnp.stack(ckv_p), jnp.stack(kr_p), jnp.stack(pool_p),
            jnp.stack(ckv_s), jnp.stack(kr_s), jnp.stack(pool_s))
```

```python
import functools
import math

import jax
import jax.numpy as jnp
from jax import lax
from jax.experimental import pallas as pl
from jax.experimental.pallas import tpu as pltpu

F32 = jnp.float32
BF16 = jnp.bfloat16

N_HEADS = 16
QK_NOPE = 64
QK_ROPE = 32
V_HEAD = 64
KV_LORA = 256
ROPE_THETA = 10000.0
ATTN_SCALE = (QK_NOPE + QK_ROPE) ** -0.5
POOL_WINDOWS = (2, 4, 8, 16)
POOL_CTX = max(POOL_WINDOWS) - 1
N_EXPERTS = 16
N_EXPERT_GROUPS = 4
EXPERTS_PER_GROUP = N_EXPERTS // N_EXPERT_GROUPS
TOP_K = 2
LN_EPS = 1e-5
RMS_EPS = 1e-6

PAIR_A = (0, 0, 0, 1, 1, 3)
PAIR_B = (1, 2, 3, 3, 2, 2)
N_PAIRS = len(PAIR_A)
N_CLASSES = N_EXPERT_GROUPS * N_PAIRS

LANES = 128
HALF_ROPE = QK_ROPE // 2
ROPE_LANE0 = QK_NOPE
HALO = 16
NEG_BIG = -1e30
LOG2E = math.log2(math.e)
VMEM_LIMIT = 52 * 1024 * 1024
MOE_TILE = 256
ATTN_TILE = 512
POOL_TILE = 512
PAGES_PER_STEP = 16
SUBLANES = 8
EXT_ROWS = 2 * SUBLANES
NT_DIMS = (((1,), (1,)), ((), ()))


def _params(*sem):
    return pltpu.CompilerParams(dimension_semantics=sem, vmem_limit_bytes=VMEM_LIMIT)


def _rms(x, g):
    return x * lax.rsqrt(jnp.mean(x * x, -1, keepdims=True) + RMS_EPS) * g


def _layer_norm(y, g, b):
    mu = jnp.mean(y, -1, keepdims=True)
    d = y - mu
    var = jnp.mean(d * d, -1, keepdims=True)
    return d * lax.rsqrt(var + LN_EPS) * g + b


def _load_chunks(ref, tokens, rows, n_chunks):
    return jnp.concatenate([ref[pl.ds(c, tokens, stride=rows), :] for c in range(n_chunks)], axis=1)


def _store_chunks(ref, val, rows):
    tokens, width = val.shape
    for c in range(width // LANES):
        ref[pl.ds(c, tokens, stride=rows), :] = val[:, c * LANES:(c + 1) * LANES]


def _rope_chunk(c, cos, sin_up, sin_dn):
    return c * cos + pltpu.roll(c, HALF_ROPE, 1) * sin_up + pltpu.roll(c, LANES - HALF_ROPE, 1) * sin_dn


def _mla_proj_kernel(x_ref, wdq_ref, gq_ref, wuq_ref, wdkv_ref, gkv_ref, wuk_ref, wuvt_ref,
                     cos_ref, sup_ref, sdn_ref, q_ref, k_ref, vt_ref, ckv_ref, kr_ref):
    xb = x_ref[...].astype(BF16)
    cos, sup, sdn = cos_ref[...], sup_ref[...], sdn_ref[...]
    cq = _rms(jnp.dot(xb, wdq_ref[...], preferred_element_type=F32), gq_ref[...])
    q = jnp.dot(cq.astype(BF16), wuq_ref[...], preferred_element_type=F32)
    kv = jnp.dot(xb, wdkv_ref[...], preferred_element_type=F32)
    ckv = _rms(kv[:, :KV_LORA], gkv_ref[...])
    ckv_ref[...] = ckv
    kr = _rope_chunk(kv[:, KV_LORA:], cos, sup, sdn)
    kr_ref[...] = kr
    ckv_b = ckv.astype(BF16)
    kn = jnp.dot(ckv_b, wuk_ref[...], preferred_element_type=F32)
    vt_ref[...] = lax.dot_general(wuvt_ref[...], ckv_b, NT_DIMS, preferred_element_type=F32).astype(BF16)
    for h in range(N_HEADS):
        sl = slice(h * LANES, (h + 1) * LANES)
        q_ref[:, sl] = (_rope_chunk(q[:, sl], cos, sup, sdn) * (ATTN_SCALE * LOG2E)).astype(BF16)
        k_ref[:, sl] = (kn[:, sl] + kr).astype(BF16)


def _mla_proj(x, wdq, gq, wuq, wdkv, gkv, wuk, wuvt, cos, sup, sdn, tm):
    n, d = x.shape
    full = lambda a: pl.BlockSpec(a.shape, lambda i: (0,) * a.ndim)
    rows = lambda w: pl.BlockSpec((tm, w), lambda i: (i, 0))
    hp = N_HEADS * LANES
    hv = N_HEADS * V_HEAD
    return pl.pallas_call(
        _mla_proj_kernel,
        grid=(n // tm,),
        in_specs=[rows(d), full(wdq), full(gq), full(wuq), full(wdkv), full(gkv), full(wuk), full(wuvt),
                  rows(LANES), rows(LANES), rows(LANES)],
        out_specs=[rows(hp), rows(hp), pl.BlockSpec((hv, tm), lambda i: (0, i)), rows(KV_LORA), rows(LANES)],
        out_shape=[jax.ShapeDtypeStruct((n, hp), BF16), jax.ShapeDtypeStruct((n, hp), BF16),
                   jax.ShapeDtypeStruct((hv, n), BF16),
                   jax.ShapeDtypeStruct((n, KV_LORA), F32), jax.ShapeDtypeStruct((n, LANES), F32)],
        compiler_params=_params("parallel"),
        name="mla_proj",
    )(x, wdq, gq, wuq, wdkv, gkv, wuk, wuvt, cos, sup, sdn)


def _flash_kernel(q_ref, k_ref, vt_ref, o_ref, *, tile):
    qi = pl.program_id(2)
    krow = lax.broadcasted_iota(jnp.int32, (tile, tile), 0)
    qcol = lax.broadcasted_iota(jnp.int32, (tile, tile), 1)

    def step(kv, carry, masked):
        start = pl.multiple_of(kv * tile, tile)
        new = []
        for hh in range(2):
            m, l, acc = carry[hh]
            hs = slice(hh * LANES, (hh + 1) * LANES)
            s = lax.dot_general(k_ref[pl.ds(start, tile), hs], q_ref[:, hs], NT_DIMS, preferred_element_type=F32)
            if masked:
                s = jnp.where(krow <= qcol, s, NEG_BIG)
            m_new = jnp.maximum(m, jnp.max(s, 0, keepdims=True))
            p = jnp.exp2(s - m_new)
            a = jnp.exp2(m - m_new)
            l = a * l + jnp.sum(p, 0, keepdims=True)
            vt = vt_ref[hh * V_HEAD:(hh + 1) * V_HEAD, pl.ds(start, tile)]
            acc = a * acc + jnp.dot(vt, p.astype(BF16), preferred_element_type=F32)
            new.append((m_new, l, acc))
        return tuple(new)

    init = tuple((jnp.full((1, tile), NEG_BIG, F32), jnp.zeros((1, tile), F32), jnp.zeros((V_HEAD, tile), F32))
                 for _ in range(2))
    carry = lax.fori_loop(0, qi, lambda kv, c: step(kv, c, False), init)
    carry = step(qi, carry, True)
    o_t = jnp.concatenate([acc / l for _, l, acc in carry], axis=0)
    o_ref[...] = o_t.T.astype(BF16)


def _flash_attention(q, k, vt, batch, seq, tile):
    n = q.shape[0]
    nq = seq // tile
    pairs = N_HEADS // 2
    return pl.pallas_call(
        functools.partial(_flash_kernel, tile=tile),
        grid=(batch, pairs, nq),
        in_specs=[pl.BlockSpec((tile, 2 * LANES), lambda b, p, i: (b * nq + i, p)),
                  pl.BlockSpec((seq, 2 * LANES), lambda b, p, i: (b, p)),
                  pl.BlockSpec((2 * V_HEAD, seq), lambda b, p, i: (p, b))],
        out_specs=pl.BlockSpec((tile, 2 * V_HEAD), lambda b, p, i: (b * nq + i, p)),
        out_shape=jax.ShapeDtypeStruct((n, N_HEADS * V_HEAD), BF16),
        compiler_params=_params("parallel", "parallel", "arbitrary"),
        name="flash_prompt",
    )(q, k, vt)


def _qlat_kernel(q_ref, w_ref, o_ref):
    o_ref[...] = jnp.dot(q_ref[...], w_ref[...], preferred_element_type=F32).astype(BF16)


def _q_latent(q, w_ukt, db):
    n = q.shape[0]
    blk = n // db - 1
    return pl.pallas_call(
        _qlat_kernel,
        grid=(N_HEADS,),
        in_specs=[pl.BlockSpec((db, LANES), lambda h: (blk, h)),
                  pl.BlockSpec((None, LANES, KV_LORA), lambda h: (h, 0, 0))],
        out_specs=pl.BlockSpec((None, db, KV_LORA), lambda h: (h, 0, 0)),
        out_shape=jax.ShapeDtypeStruct((N_HEADS, db, KV_LORA), BF16),
        compiler_params=_params("parallel"),
        name="q_latent",
    )(q, w_ukt)


def _paged_kernel(pt_ref, ql_ref, qr_ref, cnew_ref, knew_ref, ckv_hbm, krt_hbm, o_ref, kbuf, rbuf, sem,
                  *, layer, n_pages, psz, pps):
    b = pl.program_id(0)

    def page_copies(seq, slot):
        for i in range(n_pages):
            page = pt_ref[seq * n_pages + i]
            yield pltpu.make_async_copy(ckv_hbm.at[layer, page], kbuf.at[slot, pl.ds(i * psz, psz), :],
                                        sem.at[slot])
            yield pltpu.make_async_copy(krt_hbm.at[layer, page], rbuf.at[slot, :, pl.ds(i * psz, psz)],
                                        sem.at[slot])

    @pl.when(b == 0)
    def _():
        for cp in page_copies(0, 0):
            cp.start()

    @pl.when(b + 1 < pl.num_programs(0))
    def _():
        for cp in page_copies(b + 1, (b + 1) % 2):
            cp.start()

    slot = b % 2
    for cp in page_copies(b, slot):
        cp.wait()

    ql = ql_ref[...]
    qr = qr_ref[...]
    parts = []
    width = pps * psz
    for c in range(n_pages // pps):
        keys = kbuf[slot, c * width:(c + 1) * width, :].astype(BF16)
        krt = rbuf[slot, :, c * width:(c + 1) * width].astype(BF16)
        s = (lax.dot_general(ql, keys, NT_DIMS, preferred_element_type=F32)
             + jnp.dot(qr, krt, preferred_element_type=F32))
        m_c = jnp.max(s, -1, keepdims=True)
        p = jnp.exp2(s - m_c).astype(BF16)
        parts.append((m_c, jnp.sum(p.astype(F32), -1, keepdims=True),
                      jnp.dot(p, keys, preferred_element_type=F32)))
    cn = cnew_ref[...].astype(BF16).astype(F32)
    kn = knew_ref[...].astype(BF16).astype(F32)
    s_self = (jnp.sum(ql.astype(F32) * cn, -1, keepdims=True) + jnp.sum(qr.astype(F32) * kn, -1, keepdims=True))
    m = functools.reduce(jnp.maximum, [pt[0] for pt in parts] + [s_self])
    p_self = jnp.exp2(s_self - m).astype(BF16).astype(F32)
    l = p_self
    acc = p_self * cn
    for m_c, l_c, acc_c in parts:
        a = jnp.exp2(m_c - m)
        l = l + a * l_c
        acc = acc + a * acc_c
    o_ref[...] = acc / l


def _paged_attention(page_table, q_lat, q_rope, c_new, k_new, cache_ckv, cache_krt, layer, pps):
    db, n_pages = page_table.shape
    psz = cache_ckv.shape[2]
    pt = page_table.reshape(-1)
    per_b = lambda r, w: pl.BlockSpec((None, r, w), lambda b, pt_ref: (b, 0, 0))
    grid_spec = pltpu.PrefetchScalarGridSpec(
        num_scalar_prefetch=1,
        grid=(db,),
        in_specs=[per_b(N_HEADS, KV_LORA), per_b(N_HEADS, QK_ROPE), per_b(1, KV_LORA), per_b(1, QK_ROPE),
                  pl.BlockSpec(memory_space=pl.ANY), pl.BlockSpec(memory_space=pl.ANY)],
        out_specs=per_b(N_HEADS, KV_LORA),
        scratch_shapes=[pltpu.VMEM((2, n_pages * psz, KV_LORA), cache_ckv.dtype),
                        pltpu.VMEM((2, QK_ROPE, n_pages * psz), cache_krt.dtype),
                        pltpu.SemaphoreType.DMA((2,))],
    )
    return pl.pallas_call(
        functools.partial(_paged_kernel, layer=layer, n_pages=n_pages, psz=psz, pps=pps),
        grid_spec=grid_spec,
        out_shape=jax.ShapeDtypeStruct((db, N_HEADS, KV_LORA), F32),
        compiler_params=_params("arbitrary"),
        name="paged_sample",
    )(pt, q_lat, q_rope, c_new, k_new, cache_ckv, cache_krt)


def _sample_out_kernel(ol_ref, w_ref, oin_ref, o_ref):
    del oin_ref
    acc = jnp.dot(ol_ref[0].astype(BF16), w_ref[0], preferred_element_type=F32)
    acc = acc + jnp.dot(ol_ref[1].astype(BF16), w_ref[1], preferred_element_type=F32)
    o_ref[...] = acc.astype(BF16)


def _sample_out(o_lat_t, w_uv_pairs, o_all, db):
    n = o_all.shape[0]
    blk = n // db - 1
    return pl.pallas_call(
        _sample_out_kernel,
        grid=(N_HEADS // 2,),
        in_specs=[pl.BlockSpec((2, db, KV_LORA), lambda p: (p, 0, 0)),
                  pl.BlockSpec((2, KV_LORA, 2 * V_HEAD), lambda p: (p, 0, 0)),
                  pl.BlockSpec(memory_space=pl.ANY)],
        out_specs=pl.BlockSpec((db, 2 * V_HEAD), lambda p: (blk, p)),
        out_shape=jax.ShapeDtypeStruct(o_all.shape, o_all.dtype),
        input_output_aliases={2: 0},
        compiler_params=_params("parallel"),
        name="sample_out",
    )(o_lat_t, w_uv_pairs, o_all)


def _pool_prompt_kernel(x_ref, halo_ref, o_ref, l0, l1, l2, l3, *, tile, seq):
    i = pl.program_id(0)
    d = x_ref.shape[1]
    gw = d // len(POOL_WINDOWS)
    ext = tile + HALO
    s0 = (i * tile) % seq
    pos_ext = s0 - HALO + lax.broadcasted_iota(jnp.int32, (ext, 1), 0)
    zeros = jnp.zeros((HALO, d), F32)
    for buf in (l0, l1, l2, l3):
        buf[0:HALO, :] = zeros
    l0[HALO:2 * HALO, :] = halo_ref[...]
    l0[2 * HALO:, :] = x_ref[...]

    def level(src, dst, k, c0):
        cur = src[HALO:HALO + ext, c0:]
        sh = src[HALO - k:HALO - k + ext, c0:]
        dst[HALO:HALO + ext, c0:] = cur + jnp.where(pos_ext >= k, sh, 0.0)

    level(l0, l1, 1, 0)
    level(l1, l2, 2, gw)
    level(l2, l3, 4, 2 * gw)
    pos = pos_ext[HALO:]
    x = x_ref[...]
    t0 = 2 * HALO
    sums = (l1[t0:, 0:gw], l2[t0:, gw:2 * gw], l3[t0:, 2 * gw:3 * gw],
            l3[t0:, 3 * gw:] + jnp.where(pos >= 8, l3[t0 - 8:t0 - 8 + tile, 3 * gw:], 0.0))
    for g, w in enumerate(POOL_WINDOWS):
        cnt = jnp.minimum(pos + 1, w).astype(F32)
        o_ref[:, g * gw:(g + 1) * gw] = (sums[g] / cnt - x[:, g * gw:(g + 1) * gw]).astype(BF16)


def _pool_prompt(x, batch, seq, tile):
    n, d = x.shape
    hb = tile // HALO
    return pl.pallas_call(
        functools.partial(_pool_prompt_kernel, tile=tile, seq=seq),
        grid=(batch * seq // tile,),
        in_specs=[pl.BlockSpec((tile, d), lambda i: (i, 0)),
                  pl.BlockSpec((HALO, d), lambda i: (jnp.maximum(i * hb - 1, 0), 0))],
        out_specs=pl.BlockSpec((tile, d), lambda i: (i, 0)),
        out_shape=jax.ShapeDtypeStruct((n, d), BF16),
        scratch_shapes=[pltpu.VMEM((tile + 2 * HALO, d), F32)] * 4,
        compiler_params=_params("parallel"),
        name="pool_prompt",
    )(x, x)


def _pool_sample_kernel(st_ref, x_ref, pin_ref, o_ref):
    del pin_ref
    d = x_ref.shape[1]
    gw = d // len(POOL_WINDOWS)
    x = x_ref[...]
    for g, w in enumerate(POOL_WINDOWS):
        cs = slice(g * gw, (g + 1) * gw)
        acc = x[:, cs]
        for r in range(POOL_CTX - (w - 1), POOL_CTX):
            acc = acc + st_ref[r, :, cs]
        o_ref[:, cs] = (acc / float(w) - x[:, cs]).astype(BF16)


def _pool_sample(state_t, x, pooled, db):
    n, d = x.shape
    blk = n // db - 1
    return pl.pallas_call(
        _pool_sample_kernel,
        grid=(1,),
        in_specs=[pl.BlockSpec(state_t.shape, lambda i: (0, 0, 0)),
                  pl.BlockSpec((db, d), lambda i: (blk, 0)),
                  pl.BlockSpec(memory_space=pl.ANY)],
        out_specs=pl.BlockSpec((db, d), lambda i: (blk, 0)),
        out_shape=jax.ShapeDtypeStruct(pooled.shape, pooled.dtype),
        input_output_aliases={2: 0},
        compiler_params=_params("arbitrary"),
        name="pool_sample",
    )(state_t, x, pooled)


def _route_rows(logits_t, bias_ref):
    s = 1.0 / (1.0 + jnp.exp(-logits_t))
    sr = [s[e:e + 1, :] for e in range(N_EXPERTS)]
    br = [sr[e] + bias_ref[e:e + 1, :] for e in range(N_EXPERTS)]
    gscore = []
    for g in range(N_EXPERT_GROUPS):
        r = br[g * EXPERTS_PER_GROUP:(g + 1) * EXPERTS_PER_GROUP]
        best2 = None
        for a in range(EXPERTS_PER_GROUP):
            for b in range(a + 1, EXPERTS_PER_GROUP):
                pair = r[a] + r[b]
                best2 = pair if best2 is None else jnp.maximum(best2, pair)
        gscore.append(best2)
    top = functools.reduce(jnp.maximum, gscore)
    best = jnp.full(top.shape, N_EXPERT_GROUPS - 1, jnp.int32)
    for g in range(N_EXPERT_GROUPS - 2, -1, -1):
        best = jnp.where(gscore[g] == top, g, best)
    sel = []
    for e in range(N_EXPERTS):
        g = e // EXPERTS_PER_GROUP
        ahead = jnp.zeros(top.shape, F32)
        for o in range(g * EXPERTS_PER_GROUP, (g + 1) * EXPERTS_PER_GROUP):
            if o == e:
                continue
            beats = (br[o] >= br[e]) if o < e else (br[o] > br[e])
            ahead = ahead + jnp.where(beats, 1.0, 0.0)
        sel.append(jnp.where(ahead < float(TOP_K), 1.0, 0.0) * jnp.where(best == g, 1.0, 0.0))
    den = functools.reduce(lambda a, b: a + b, [sel[e] * sr[e] for e in range(N_EXPERTS)])
    wts = [sel[e] * sr[e] / den for e in range(N_EXPERTS)]
    return sel, wts


def _post_mix_kernel(a_ref, x_ref, w_ref, sc_ref, g_ref, b_ref, wrh_ref, wrl_ref, rb_ref, tri_ref,
                     xe_ref, cls_ref, rank_ref, cnt_ref, cnt_scr, *, alpha):
    i = pl.program_id(0)
    d = x_ref.shape[1]

    @pl.when(i == 0)
    def _():
        cnt_scr[...] = jnp.zeros(cnt_scr.shape, F32)

    mix = jnp.dot(a_ref[...], w_ref[...], preferred_element_type=F32) * sc_ref[...]
    x1 = _layer_norm(alpha * x_ref[...] + mix, g_ref[...], b_ref[...])
    _store_chunks(xe_ref, x1, EXT_ROWS)
    x_hi = x1.astype(BF16)
    x_lo = (x1 - x_hi.astype(F32)).astype(BF16)
    wrh = wrh_ref[...]
    logits_t = (lax.dot_general(wrh, x_hi, NT_DIMS, preferred_element_type=F32)
                + lax.dot_general(wrh, x_lo, NT_DIMS, preferred_element_type=F32)
                + lax.dot_general(wrl_ref[...], x_hi, NT_DIMS, preferred_element_type=F32))
    sel, wts = _route_rows(logits_t, rb_ref)
    t = logits_t.shape[1]
    ind = []
    w_a = jnp.zeros(sel[0].shape, F32)
    w_b = jnp.zeros(sel[0].shape, F32)
    cls = jnp.zeros(sel[0].shape, F32)
    for g in range(N_EXPERT_GROUPS):
        for k in range(N_PAIRS):
            ea = g * EXPERTS_PER_GROUP + PAIR_A[k]
            eb = g * EXPERTS_PER_GROUP + PAIR_B[k]
            hit = sel[ea] * sel[eb]
            w_a = w_a + hit * wts[ea]
            w_b = w_b + hit * wts[eb]
            cls = cls + hit * float(len(ind))
            ind.append(hit)
    crow = lax.broadcasted_iota(jnp.int32, (N_CLASSES, t), 0)
    ind_t = jnp.zeros((N_CLASSES, t), F32)
    for c in range(N_CLASSES):
        ind_t = jnp.where(crow == c, ind[c], ind_t)
    before = jnp.dot(ind_t.astype(BF16), tri_ref[...], preferred_element_type=F32)
    rank_t = cnt_scr[...] + before
    cnt_new = cnt_scr[...] + jnp.sum(ind_t, -1, keepdims=True)
    cnt_scr[...] = cnt_new
    cnt_ref[...] = cnt_new
    rank = jnp.zeros(sel[0].shape, F32)
    for c in range(N_CLASSES):
        rank = rank + ind[c] * rank_t[c:c + 1, :]
    cls_ref[...] = cls.astype(jnp.int32)
    rank_ref[...] = rank.astype(jnp.int32)
    lrow = lax.broadcasted_iota(jnp.int32, (LANES, t), 0)
    w_t = jnp.where(lrow == 0, w_a, jnp.where(lrow == 1, w_b, 0.0))
    n_chunks = d // LANES
    xe_ref[pl.ds(n_chunks, t, stride=EXT_ROWS), :] = w_t.T
    for c in range(n_chunks + 1, EXT_ROWS):
        xe_ref[pl.ds(c, t, stride=EXT_ROWS), :] = jnp.zeros((t, LANES), F32)


def _post_mix(a, x, w, scale, g, b, wr_hi, wr_lo, rbias, tri, alpha, tm):
    n, d = x.shape
    full = lambda t: pl.BlockSpec(t.shape, lambda i: (0,) * t.ndim)
    rows = lambda wd: pl.BlockSpec((tm, wd), lambda i: (i, 0))
    cols = pl.BlockSpec((1, tm), lambda i: (0, i))
    return pl.pallas_call(
        functools.partial(_post_mix_kernel, alpha=alpha),
        grid=(n // tm,),
        in_specs=[rows(d), rows(d), full(w), full(scale), full(g), full(b), full(wr_hi), full(wr_lo),
                  full(rbias), full(tri)],
        out_specs=[pl.BlockSpec((tm * EXT_ROWS, LANES), lambda i: (i, 0)), cols, cols,
                   pl.BlockSpec((N_CLASSES, 1), lambda i: (0, 0))],
        out_shape=[jax.ShapeDtypeStruct((n * EXT_ROWS, LANES), F32),
                   jax.ShapeDtypeStruct((1, n), jnp.int32), jax.ShapeDtypeStruct((1, n), jnp.int32),
                   jax.ShapeDtypeStruct((N_CLASSES, 1), F32)],
        scratch_shapes=[pltpu.VMEM((N_CLASSES, 1), F32)],
        compiler_params=_params("arbitrary"),
        name="post_mix",
    )(a, x, w, scale, g, b, wr_hi, wr_lo, rbias, tri)


def _token_copy(src_ref, src_tok, dst_ref, dst_tok, rows, sem):
    src = src_ref.at[pl.ds(pl.multiple_of(src_tok * rows, rows), rows), :]
    dst = dst_ref.at[pl.ds(pl.multiple_of(dst_tok * rows, rows), rows), :]
    return pltpu.make_async_copy(src, dst, sem)


def _dispatch_kernel(pos_ref, x_ref, xs_in_ref, xs_ref, sem, *, tm):
    del xs_in_ref
    base = pl.program_id(0) * tm

    def start(r, carry):
        _token_copy(x_ref, r, xs_ref, pos_ref[base + r], EXT_ROWS, sem).start()
        return carry

    def wait(r, carry):
        _token_copy(x_ref, r, xs_ref, pos_ref[base + r], EXT_ROWS, sem).wait()
        return carry

    lax.fori_loop(0, tm, start, 0)
    lax.fori_loop(0, tm, wait, 0)


def _dispatch(pos, xe, xs_init, tm):
    n = xe.shape[0] // EXT_ROWS
    grid_spec = pltpu.PrefetchScalarGridSpec(
        num_scalar_prefetch=1,
        grid=(n // tm,),
        in_specs=[pl.BlockSpec((tm * EXT_ROWS, LANES), lambda i, p: (i, 0)), pl.BlockSpec(memory_space=pl.ANY)],
        out_specs=pl.BlockSpec(memory_space=pl.ANY),
        scratch_shapes=[pltpu.SemaphoreType.DMA(())],
    )
    return pl.pallas_call(
        functools.partial(_dispatch_kernel, tm=tm),
        grid_spec=grid_spec,
        out_shape=jax.ShapeDtypeStruct(xs_init.shape, xs_init.dtype),
        input_output_aliases={2: 0},
        compiler_params=_params("arbitrary"),
        name="moe_dispatch",
    )(pos, xe, xs_init)


def _unpermute_kernel(pos_ref, xs_ref, o_ref, buf, sem, *, tm):
    base = pl.program_id(0) * tm

    rows = o_ref.shape[1] // LANES

    def start(r, carry):
        _token_copy(xs_ref, pos_ref[base + r], buf, r, rows, sem).start()
        return carry

    def wait(r, carry):
        _token_copy(xs_ref, pos_ref[base + r], buf, r, rows, sem).wait()
        return carry

    lax.fori_loop(0, tm, start, 0)
    lax.fori_loop(0, tm, wait, 0)
    o_ref[...] = _load_chunks(buf, tm, rows, rows)


def _unpermute(pos, xs, n, d, tm):
    rows = d // LANES
    grid_spec = pltpu.PrefetchScalarGridSpec(
        num_scalar_prefetch=1,
        grid=(n // tm,),
        in_specs=[pl.BlockSpec(memory_space=pl.ANY)],
        out_specs=pl.BlockSpec((tm, d), lambda i, p: (i, 0)),
        scratch_shapes=[pltpu.VMEM((tm * rows, LANES), xs.dtype), pltpu.SemaphoreType.DMA(())],
    )
    return pl.pallas_call(
        functools.partial(_unpermute_kernel, tm=tm),
        grid_spec=grid_spec,
        out_shape=jax.ShapeDtypeStruct((n, d), xs.dtype),
        compiler_params=_params("arbitrary"),
        name="moe_unpermute",
    )(pos, xs)


def _moe_kernel(ta_ref, tb_ref, nu_ref, x_ref, wga_ref, wua_ref, wda_ref, wgb_ref, wub_ref, wdb_ref,
                g_ref, b_ref, o_ref, wga_s, wua_s, wda_s, wgb_s, wub_s, wdb_s, *, alpha, tile):
    i = pl.program_id(0)
    n_chunks = o_ref.shape[0] // tile

    @pl.when(i < nu_ref[0])
    def _():
        prev = jnp.maximum(i - 1, 0)

        @pl.when((i == 0) | (ta_ref[i] != ta_ref[prev]))
        def _():
            wga_s[...] = wga_ref[...].astype(BF16)
            wua_s[...] = wua_ref[...].astype(BF16)
            wda_s[...] = wda_ref[...].astype(BF16)

        @pl.when((i == 0) | (tb_ref[i] != tb_ref[prev]))
        def _():
            wgb_s[...] = wgb_ref[...].astype(BF16)
            wub_s[...] = wub_ref[...].astype(BF16)
            wdb_s[...] = wdb_ref[...].astype(BF16)

        x = _load_chunks(x_ref, tile, EXT_ROWS, n_chunks)
        combine = x_ref[pl.ds(n_chunks, tile, stride=EXT_ROWS), :]
        xb = x.astype(BF16)
        y = None
        for col, (wg, wu, wd) in enumerate(((wga_s, wua_s, wda_s), (wgb_s, wub_s, wdb_s))):
            gate = jnp.dot(xb, wg[...], preferred_element_type=F32)
            up = jnp.dot(xb, wu[...], preferred_element_type=F32)
            h = gate * (1.0 / (1.0 + jnp.exp(-gate))) * up * combine[:, col:col + 1]
            part = jnp.dot(h.astype(BF16), wd[...], preferred_element_type=F32)
            y = part if y is None else y + part
        _store_chunks(o_ref, _layer_norm(alpha * x + y, g_ref[...], b_ref[...]), n_chunks)


def _moe(tile_a, tile_b, n_used, xs, w_gate, w_up, w_down, layer, g, b, alpha, tile):
    p = xs.shape[0] // EXT_ROWS
    d, f = w_gate.shape[-2:]
    ch = d // LANES
    row_blk = lambda i, ta, tb, nu: (jnp.minimum(i, nu[0] - 1), 0)
    wa = lambda r, c: pl.BlockSpec((None, None, r, c), lambda i, ta, tb, nu: (layer, ta[i], 0, 0))
    wb = lambda r, c: pl.BlockSpec((None, None, r, c), lambda i, ta, tb, nu: (layer, tb[i], 0, 0))
    full = lambda t: pl.BlockSpec(t.shape, lambda i, ta, tb, nu: (0,) * t.ndim)
    grid_spec = pltpu.PrefetchScalarGridSpec(
        num_scalar_prefetch=3,
        grid=(p // tile,),
        in_specs=[pl.BlockSpec((tile * EXT_ROWS, LANES), row_blk), wa(d, f), wa(d, f), wa(f, d),
                  wb(d, f), wb(d, f), wb(f, d), full(g), full(b)],
        out_specs=pl.BlockSpec((tile * ch, LANES), row_blk),
        scratch_shapes=[pltpu.VMEM((d, f), BF16), pltpu.VMEM((d, f), BF16), pltpu.VMEM((f, d), BF16)] * 2,
    )
    return pl.pallas_call(
        functools.partial(_moe_kernel, alpha=alpha, tile=tile),
        grid_spec=grid_spec,
        out_shape=jax.ShapeDtypeStruct((p * ch, LANES), F32),
        compiler_params=_params("arbitrary"),
        name="moe_grouped",
    )(tile_a, tile_b, n_used, xs, w_gate, w_up, w_down, w_gate, w_up, w_down, g, b)


def _dispatch_tables(cls, rank, counts, n_tiles):
    padded = ((counts + MOE_TILE - 1) // MOE_TILE) * MOE_TILE
    ends = jnp.cumsum(padded)
    pos = (ends - padded)[cls] + rank
    tile_start = jnp.arange(n_tiles, dtype=jnp.int32) * MOE_TILE
    tile_cls = jnp.minimum(jnp.sum((tile_start[:, None] >= ends[None, :]).astype(jnp.int32), 1), N_CLASSES - 1)
    group0 = (tile_cls // N_PAIRS) * EXPERTS_PER_GROUP
    tile_a = group0 + jnp.asarray(PAIR_A, jnp.int32)[tile_cls % N_PAIRS]
    tile_b = group0 + jnp.asarray(PAIR_B, jnp.int32)[tile_cls % N_PAIRS]
    n_used = (ends[-1:] // MOE_TILE).astype(jnp.int32)
    return pos.astype(jnp.int32), tile_a.astype(jnp.int32), tile_b.astype(jnp.int32), n_used


def _rope_tables(pos):
    inv = ROPE_THETA ** (-2.0 * jnp.arange(HALF_ROPE, dtype=F32) / QK_ROPE)
    ang = pos.astype(F32)[:, None] * inv[None, :]
    cos, sin = jnp.cos(ang), jnp.sin(ang)
    n = pos.shape[0]
    ones = jnp.ones((n, ROPE_LANE0), F32)
    zeros = jnp.zeros((n, ROPE_LANE0), F32)
    tail1 = jnp.ones((n, LANES - ROPE_LANE0 - QK_ROPE), F32)
    tail0 = jnp.zeros((n, LANES - ROPE_LANE0 - QK_ROPE), F32)
    zh = jnp.zeros((n, HALF_ROPE), F32)
    cos_t = jnp.concatenate([ones, cos, cos, tail1], 1)
    sin_up = jnp.concatenate([zeros, zh, sin, tail0], 1)
    sin_dn = jnp.concatenate([zeros, -sin, zh, tail0], 1)
    return cos_t, sin_up, sin_dn


def _pad_heads(w, per_head):
    k = w.shape[0]
    w = w.reshape(k, N_HEADS, per_head)
    return jnp.pad(w, ((0, 0), (0, 0), (0, LANES - per_head))).reshape(k, N_HEADS * LANES)


def kernel(x_prompt, x_sample, cache_ckv, cache_krope, state_pool, page_table, w_dq, g_q, w_uq, w_dkv, g_kv,
           w_uk, w_uv, w_o, w_pool, pool_scale, w_router, router_bias, w_gate, w_up, w_down,
           ln1_g, ln1_b, ln2_g, ln2_b):
    batch, seq, d = x_prompt.shape
    db, dec_t, _ = x_sample.shape
    assert dec_t == 1 and d % LANES == 0 and seq % ATTN_TILE == 0 and seq % POOL_TILE == 0
    depth = ln1_g.shape[0]
    alpha = (2 * depth) ** 0.25
    n_prompt = batch * seq
    n = n_prompt + db
    assert n % db == 0 and db % 8 == 0
    tm = 384 if n % 384 == 0 else 128
    assert n % tm == 0
    n_pages = page_table.shape[1]
    past_len = n_pages * cache_ckv.shape[2]
    pps = min(PAGES_PER_STEP, n_pages)
    assert n_pages % pps == 0
    n_moe_tiles = (n + N_CLASSES * (MOE_TILE - 1)) // MOE_TILE
    row = lambda v: v.reshape(1, -1).astype(F32)

    pos = jnp.concatenate([jnp.tile(jnp.arange(seq), batch), jnp.full((db,), past_len)])
    cos_t, sin_up, sin_dn = _rope_tables(pos)
    tri = (jnp.arange(tm)[:, None] < jnp.arange(tm)[None, :]).astype(BF16)
    wr_t = w_router.T.astype(F32)
    wr_hi = wr_t.astype(BF16)
    wr_lo = (wr_t - wr_hi.astype(F32)).astype(BF16)
    rbias = router_bias.reshape(N_EXPERTS, 1).astype(F32)
    ones_d = jnp.ones((1, d), F32)
    cache_krt = jnp.swapaxes(cache_krope, 2, 3)
    assert d // LANES < EXT_ROWS
    xs_init = jnp.zeros((n_moe_tiles * MOE_TILE * EXT_ROWS, LANES), F32)

    x = jnp.concatenate([x_prompt.reshape(n_prompt, d), x_sample.reshape(db, d)], 0)
    ckv_p, kr_p, pool_p, ckv_s, kr_s, pool_s = [], [], [], [], [], []
    for i in range(depth):
        j = i // 2
        if i % 2 == 0:
            wuq_p = _pad_heads(w_uq[j], QK_NOPE + QK_ROPE).astype(BF16)
            wdkv_p = jnp.concatenate(
                [w_dkv[j][:, :KV_LORA], jnp.zeros((d, ROPE_LANE0), F32), w_dkv[j][:, KV_LORA:],
                 jnp.zeros((d, LANES - ROPE_LANE0 - QK_ROPE), F32)], 1).astype(BF16)
            wuk_p = _pad_heads(w_uk[j].reshape(KV_LORA, N_HEADS * QK_NOPE), QK_NOPE).astype(BF16)
            wuv_t = w_uv[j].reshape(KV_LORA, N_HEADS * V_HEAD).T.astype(BF16)
            q_all, k_all, vt_all, ckv_all, kr_all = _mla_proj(
                x, w_dq[j].astype(BF16), row(g_q[j]), wuq_p, wdkv_p, row(g_kv[j]), wuk_p, wuv_t,
                cos_t, sin_up, sin_dn, tm)
            kr_all = kr_all[:, ROPE_LANE0:ROPE_LANE0 + QK_ROPE]
            o_all = _flash_attention(q_all, k_all, vt_all, batch, seq, ATTN_TILE)
            w_ukt = jnp.pad(jnp.transpose(w_uk[j], (1, 2, 0)), ((0, 0), (0, LANES - QK_NOPE), (0, 0))).astype(BF16)
            q_lat = jnp.transpose(_q_latent(q_all, w_ukt, db), (1, 0, 2))
            q_rope = q_all[n_prompt:].reshape(db, N_HEADS, LANES)[:, :, ROPE_LANE0:ROPE_LANE0 + QK_ROPE]
            o_lat = _paged_attention(page_table, q_lat, q_rope, ckv_all[n_prompt:].reshape(db, 1, KV_LORA),
                                     kr_all[n_prompt:].reshape(db, 1, QK_ROPE), cache_ckv, cache_krt, j, pps)
            wv = jnp.transpose(w_uv[j], (1, 0, 2))
            wv_even = jnp.pad(wv, ((0, 0), (0, 0), (0, V_HEAD)))
            wv_odd = jnp.pad(wv, ((0, 0), (0, 0), (V_HEAD, 0)))
            wv_pairs = jnp.where((jnp.arange(N_HEADS) % 2 == 0)[:, None, None], wv_even, wv_odd).astype(BF16)
            mixed = _sample_out(jnp.transpose(o_lat, (1, 0, 2)), wv_pairs, o_all, db)
            w_mix, mix_scale = w_o[j].astype(BF16), ones_d
            ckv_p.append(ckv_all[:n_prompt].reshape(batch, seq, KV_LORA))
            kr_p.append(kr_all[:n_prompt].reshape(batch, seq, QK_ROPE))
            ckv_s.append(ckv_all[n_prompt:].reshape(db, 1, KV_LORA))
            kr_s.append(kr_all[n_prompt:].reshape(db, 1, QK_ROPE))
        else:
            pooled = _pool_prompt(x, batch, seq, POOL_TILE)
            mixed = _pool_sample(jnp.transpose(state_pool[j], (1, 0, 2)).astype(F32), x, pooled, db)
            w_mix = jax.scipy.linalg.block_diag(*[w_pool[j][g] for g in range(len(POOL_WINDOWS))]).astype(BF16)
            mix_scale = row(pool_scale[j])
            pool_p.append(x[:n_prompt].reshape(batch, seq, d)[:, seq - POOL_CTX:])
            pool_s.append(jnp.concatenate([state_pool[j][:, 1:].astype(F32), x[n_prompt:, None, :]], 1))
        xe, cls, rank, counts = _post_mix(
            mixed, x, w_mix, mix_scale, row(ln1_g[i]), row(ln1_b[i]), wr_hi, wr_lo, rbias, tri, alpha, tm)
        row_pos, tile_a, tile_b, n_used = _dispatch_tables(
            cls.reshape(-1), rank.reshape(-1), counts.reshape(-1).astype(jnp.int32), n_moe_tiles)
        xs = _dispatch(row_pos, xe, xs_init, tm)
        ys = _moe(tile_a, tile_b, n_used, xs, w_gate, w_up, w_down, i,
                  row(ln2_g[i]), row(ln2_b[i]), alpha, MOE_TILE)
        x = _unpermute(row_pos, ys, n, d, tm)
    return (x[:n_prompt].reshape(batch, seq, d), x[n_prompt:].reshape(db, 1, d),
            jnp.stack(ckv_p), jnp.stack(kr_p), jnp.stack(pool_p),
            jnp.stack(ckv_s), jnp.stack(kr_s), jnp.stack(pool_s))
```

```python
import functools
import math

import jax
import jax.numpy as jnp
from jax import lax
from jax.experimental import pallas as pl
from jax.experimental.pallas import tpu as pltpu

F32 = jnp.float32
BF16 = jnp.bfloat16

N_HEADS = 16
QK_NOPE = 64
QK_ROPE = 32
V_HEAD = 64
KV_LORA = 256
ROPE_THETA = 10000.0
ATTN_SCALE = (QK_NOPE + QK_ROPE) ** -0.5
POOL_WINDOWS = (2, 4, 8, 16)
POOL_CTX = max(POOL_WINDOWS) - 1
N_EXPERTS = 16
N_EXPERT_GROUPS = 4
EXPERTS_PER_GROUP = N_EXPERTS // N_EXPERT_GROUPS
TOP_K = 2
LN_EPS = 1e-5
RMS_EPS = 1e-6

PAIR_A = (0, 0, 0, 1, 1, 3)
PAIR_B = (1, 2, 3, 3, 2, 2)
N_PAIRS = len(PAIR_A)
N_CLASSES = N_EXPERT_GROUPS * N_PAIRS

LANES = 128
HALF_ROPE = QK_ROPE // 2
ROPE_LANE0 = QK_NOPE
HALO = 16
NEG_BIG = -1e30
LOG2E = math.log2(math.e)
VMEM_LIMIT = 52 * 1024 * 1024
MOE_TILE = 256
ATTN_TILE = 512
ATTN_HEADS_PER_STEP = 4
ONES_ROWS = 16
POOL_TILE = 512
PAGES_PER_STEP = 16
SUBLANES = 8
EXT_ROWS = 2 * SUBLANES
NT_DIMS = (((1,), (1,)), ((), ()))


def _params(*sem):
    return pltpu.CompilerParams(dimension_semantics=sem, vmem_limit_bytes=VMEM_LIMIT)


def _rms(x, g):
    return x * lax.rsqrt(jnp.mean(x * x, -1, keepdims=True) + RMS_EPS) * g


def _layer_norm(y, g, b):
    mu = jnp.mean(y, -1, keepdims=True)
    d = y - mu
    var = jnp.mean(d * d, -1, keepdims=True)
    return d * lax.rsqrt(var + LN_EPS) * g + b


def _load_chunks(ref, tokens, rows, n_chunks):
    return jnp.concatenate([ref[pl.ds(c, tokens, stride=rows), :] for c in range(n_chunks)], axis=1)


def _store_chunks(ref, val, rows):
    tokens, width = val.shape
    for c in range(width // LANES):
        ref[pl.ds(c, tokens, stride=rows), :] = val[:, c * LANES:(c + 1) * LANES]


def _rope_chunk(c, cos, sin_up, sin_dn):
    return c * cos + pltpu.roll(c, HALF_ROPE, 1) * sin_up + pltpu.roll(c, LANES - HALF_ROPE, 1) * sin_dn


def _mla_proj_kernel(x_ref, wdq_ref, gq_ref, wuq_ref, wdkv_ref, gkv_ref, wuk_ref, wuvt_ref,
                     cos_ref, sup_ref, sdn_ref, q_ref, k_ref, vt_ref, ckv_ref, kr_ref):
    xb = x_ref[...].astype(BF16)
    cos, sup, sdn = cos_ref[...], sup_ref[...], sdn_ref[...]
    cq = _rms(jnp.dot(xb, wdq_ref[...], preferred_element_type=F32), gq_ref[...])
    q = jnp.dot(cq.astype(BF16), wuq_ref[...], preferred_element_type=F32)
    kv = jnp.dot(xb, wdkv_ref[...], preferred_element_type=F32)
    ckv = _rms(kv[:, :KV_LORA], gkv_ref[...])
    ckv_ref[...] = ckv
    kr = _rope_chunk(kv[:, KV_LORA:], cos, sup, sdn)
    kr_ref[...] = kr
    ckv_b = ckv.astype(BF16)
    kn = jnp.dot(ckv_b, wuk_ref[...], preferred_element_type=F32)
    vt_ref[...] = lax.dot_general(wuvt_ref[...], ckv_b, NT_DIMS, preferred_element_type=F32).astype(BF16)
    for h in range(N_HEADS):
        sl = slice(h * LANES, (h + 1) * LANES)
        q_ref[:, sl] = (_rope_chunk(q[:, sl], cos, sup, sdn) * (ATTN_SCALE * LOG2E)).astype(BF16)
        k_ref[:, sl] = (kn[:, sl] + kr).astype(BF16)


def _mla_proj(x, wdq, gq, wuq, wdkv, gkv, wuk, wuvt, cos, sup, sdn, tm):
    n, d = x.shape
    full = lambda a: pl.BlockSpec(a.shape, lambda i: (0,) * a.ndim)
    rows = lambda w: pl.BlockSpec((tm, w), lambda i: (i, 0))
    hp = N_HEADS * LANES
    hv = N_HEADS * V_HEAD
    return pl.pallas_call(
        _mla_proj_kernel,
        grid=(n // tm,),
        in_specs=[rows(d), full(wdq), full(gq), full(wuq), full(wdkv), full(gkv), full(wuk), full(wuvt),
                  rows(LANES), rows(LANES), rows(LANES)],
        out_specs=[rows(hp), rows(hp), pl.BlockSpec((hv, tm), lambda i: (0, i)), rows(KV_LORA), rows(LANES)],
        out_shape=[jax.ShapeDtypeStruct((n, hp), BF16), jax.ShapeDtypeStruct((n, hp), BF16),
                   jax.ShapeDtypeStruct((hv, n), BF16),
                   jax.ShapeDtypeStruct((n, KV_LORA), F32), jax.ShapeDtypeStruct((n, LANES), F32)],
        compiler_params=_params("parallel"),
        name="mla_proj",
    )(x, wdq, gq, wuq, wdkv, gkv, wuk, wuvt, cos, sup, sdn)


def _flash_kernel(q_ref, k_ref, vt_ref, o_ref, *, tile, heads):
    qi = pl.program_id(2)
    krow = lax.broadcasted_iota(jnp.int32, (tile, tile), 0)
    qcol = lax.broadcasted_iota(jnp.int32, (tile, tile), 1)
    ones = jnp.ones((ONES_ROWS, tile), BF16)

    def step(kv, carry, masked):
        start = pl.multiple_of(kv * tile, tile)
        new = []
        scores = [lax.dot_general(k_ref[pl.ds(start, tile), hh * LANES:(hh + 1) * LANES],
                                  q_ref[:, hh * LANES:(hh + 1) * LANES], NT_DIMS, preferred_element_type=F32)
                  for hh in range(heads)]
        for hh in range(heads):
            m, acc = carry[hh]
            s = scores[hh]
            if masked:
                s = jnp.where(krow <= qcol, s, NEG_BIG)
            m_new = jnp.maximum(m, jnp.max(s, 0, keepdims=True))
            p = jnp.exp2(s - m_new).astype(BF16)
            a = jnp.exp2(m - m_new)
            vt = jnp.concatenate([vt_ref[hh * V_HEAD:(hh + 1) * V_HEAD, pl.ds(start, tile)], ones], axis=0)
            acc = a * acc + jnp.dot(vt, p, preferred_element_type=F32)
            new.append((m_new, acc))
        return tuple(new)

    init = tuple((jnp.full((1, tile), NEG_BIG, F32), jnp.zeros((V_HEAD + ONES_ROWS, tile), F32))
                 for _ in range(heads))
    carry = lax.fori_loop(0, qi, lambda kv, c: step(kv, c, False), init)
    carry = step(qi, carry, True)
    o_t = jnp.concatenate([acc[:V_HEAD] / acc[V_HEAD:V_HEAD + 1] for _, acc in carry], axis=0)
    o_ref[...] = o_t.T.astype(BF16)


def _flash_attention(q, k, vt, batch, seq, tile):
    n = q.shape[0]
    nq = seq // tile
    hps = ATTN_HEADS_PER_STEP
    return pl.pallas_call(
        functools.partial(_flash_kernel, tile=tile, heads=hps),
        grid=(batch, N_HEADS // hps, nq),
        in_specs=[pl.BlockSpec((tile, hps * LANES), lambda b, p, i: (b * nq + i, p)),
                  pl.BlockSpec((seq, hps * LANES), lambda b, p, i: (b, p)),
                  pl.BlockSpec((hps * V_HEAD, seq), lambda b, p, i: (p, b))],
        out_specs=pl.BlockSpec((tile, hps * V_HEAD), lambda b, p, i: (b * nq + i, p)),
        out_shape=jax.ShapeDtypeStruct((n, N_HEADS * V_HEAD), BF16),
        compiler_params=_params("parallel", "parallel", "arbitrary"),
        name="flash_prompt",
    )(q, k, vt)


def _qlat_kernel(q_ref, w_ref, o_ref):
    o_ref[...] = jnp.dot(q_ref[...], w_ref[...], preferred_element_type=F32).astype(BF16)


def _q_latent(q, w_ukt, db):
    n = q.shape[0]
    blk = n // db - 1
    return pl.pallas_call(
        _qlat_kernel,
        grid=(N_HEADS,),
        in_specs=[pl.BlockSpec((db, LANES), lambda h: (blk, h)),
                  pl.BlockSpec((None, LANES, KV_LORA), lambda h: (h, 0, 0))],
        out_specs=pl.BlockSpec((None, db, KV_LORA), lambda h: (h, 0, 0)),
        out_shape=jax.ShapeDtypeStruct((N_HEADS, db, KV_LORA), BF16),
        compiler_params=_params("parallel"),
        name="q_latent",
    )(q, w_ukt)


def _paged_kernel(pt_ref, ql_ref, qr_ref, cnew_ref, knew_ref, ckv_hbm, krt_hbm, o_ref, kbuf, rbuf, sem,
                  *, layer, n_pages, psz, pps):
    b = pl.program_id(0)

    def page_copies(seq, slot):
        for i in range(n_pages):
            page = pt_ref[seq * n_pages + i]
            yield pltpu.make_async_copy(ckv_hbm.at[layer, page], kbuf.at[slot, pl.ds(i * psz, psz), :],
                                        sem.at[slot])
            yield pltpu.make_async_copy(krt_hbm.at[layer, page], rbuf.at[slot, :, pl.ds(i * psz, psz)],
                                        sem.at[slot])

    @pl.when(b == 0)
    def _():
        for cp in page_copies(0, 0):
            cp.start()

    @pl.when(b + 1 < pl.num_programs(0))
    def _():
        for cp in page_copies(b + 1, (b + 1) % 2):
            cp.start()

    slot = b % 2
    for cp in page_copies(b, slot):
        cp.wait()

    ql = ql_ref[...]
    qr = qr_ref[...]
    parts = []
    width = pps * psz
    for c in range(n_pages // pps):
        keys = kbuf[slot, c * width:(c + 1) * width, :].astype(BF16)
        krt = rbuf[slot, :, c * width:(c + 1) * width].astype(BF16)
        s = (lax.dot_general(ql, keys, NT_DIMS, preferred_element_type=F32)
             + jnp.dot(qr, krt, preferred_element_type=F32))
        m_c = jnp.max(s, -1, keepdims=True)
        p = jnp.exp2(s - m_c).astype(BF16)
        parts.append((m_c, jnp.sum(p.astype(F32), -1, keepdims=True),
                      jnp.dot(p, keys, preferred_element_type=F32)))
    cn = cnew_ref[...].astype(BF16).astype(F32)
    kn = knew_ref[...].astype(BF16).astype(F32)
    s_self = (jnp.sum(ql.astype(F32) * cn, -1, keepdims=True) + jnp.sum(qr.astype(F32) * kn, -1, keepdims=True))
    m = functools.reduce(jnp.maximum, [pt[0] for pt in parts] + [s_self])
    p_self = jnp.exp2(s_self - m).astype(BF16).astype(F32)
    l = p_self
    acc = p_self * cn
    for m_c, l_c, acc_c in parts:
        a = jnp.exp2(m_c - m)
        l = l + a * l_c
        acc = acc + a * acc_c
    o_ref[...] = acc / l


def _paged_attention(page_table, q_lat, q_rope, c_new, k_new, cache_ckv, cache_krt, layer, pps):
    db, n_pages = page_table.shape
    psz = cache_ckv.shape[2]
    pt = page_table.reshape(-1)
    per_b = lambda r, w: pl.BlockSpec((None, r, w), lambda b, pt_ref: (b, 0, 0))
    grid_spec = pltpu.PrefetchScalarGridSpec(
        num_scalar_prefetch=1,
        grid=(db,),
        in_specs=[per_b(N_HEADS, KV_LORA), per_b(N_HEADS, QK_ROPE), per_b(1, KV_LORA), per_b(1, QK_ROPE),
                  pl.BlockSpec(memory_space=pl.ANY), pl.BlockSpec(memory_space=pl.ANY)],
        out_specs=per_b(N_HEADS, KV_LORA),
        scratch_shapes=[pltpu.VMEM((2, n_pages * psz, KV_LORA), cache_ckv.dtype),
                        pltpu.VMEM((2, QK_ROPE, n_pages * psz), cache_krt.dtype),
                        pltpu.SemaphoreType.DMA((2,))],
    )
    return pl.pallas_call(
        functools.partial(_paged_kernel, layer=layer, n_pages=n_pages, psz=psz, pps=pps),
        grid_spec=grid_spec,
        out_shape=jax.ShapeDtypeStruct((db, N_HEADS, KV_LORA), F32),
        compiler_params=_params("arbitrary"),
        name="paged_sample",
    )(pt, q_lat, q_rope, c_new, k_new, cache_ckv, cache_krt)


def _sample_out_kernel(ol_ref, w_ref, oin_ref, o_ref):
    del oin_ref
    acc = jnp.dot(ol_ref[0].astype(BF16), w_ref[0], preferred_element_type=F32)
    acc = acc + jnp.dot(ol_ref[1].astype(BF16), w_ref[1], preferred_element_type=F32)
    o_ref[...] = acc.astype(BF16)


def _sample_out(o_lat_t, w_uv_pairs, o_all, db):
    n = o_all.shape[0]
    blk = n // db - 1
    return pl.pallas_call(
        _sample_out_kernel,
        grid=(N_HEADS // 2,),
        in_specs=[pl.BlockSpec((2, db, KV_LORA), lambda p: (p, 0, 0)),
                  pl.BlockSpec((2, KV_LORA, 2 * V_HEAD), lambda p: (p, 0, 0)),
                  pl.BlockSpec(memory_space=pl.ANY)],
        out_specs=pl.BlockSpec((db, 2 * V_HEAD), lambda p: (blk, p)),
        out_shape=jax.ShapeDtypeStruct(o_all.shape, o_all.dtype),
        input_output_aliases={2: 0},
        compiler_params=_params("parallel"),
        name="sample_out",
    )(o_lat_t, w_uv_pairs, o_all)


def _pool_prompt_kernel(x_ref, halo_ref, o_ref, l0, l1, l2, l3, *, tile, seq):
    i = pl.program_id(0)
    d = x_ref.shape[1]
    gw = d // len(POOL_WINDOWS)
    ext = tile + HALO
    s0 = (i * tile) % seq
    pos_ext = s0 - HALO + lax.broadcasted_iota(jnp.int32, (ext, 1), 0)
    zeros = jnp.zeros((HALO, d), F32)
    for buf in (l0, l1, l2, l3):
        buf[0:HALO, :] = zeros
    l0[HALO:2 * HALO, :] = halo_ref[...]
    l0[2 * HALO:, :] = x_ref[...]

    def level(src, dst, k, c0):
        cur = src[HALO:HALO + ext, c0:]
        sh = src[HALO - k:HALO - k + ext, c0:]
        dst[HALO:HALO + ext, c0:] = cur + jnp.where(pos_ext >= k, sh, 0.0)

    level(l0, l1, 1, 0)
    level(l1, l2, 2, gw)
    level(l2, l3, 4, 2 * gw)
    pos = pos_ext[HALO:]
    x = x_ref[...]
    t0 = 2 * HALO
    sums = (l1[t0:, 0:gw], l2[t0:, gw:2 * gw], l3[t0:, 2 * gw:3 * gw],
            l3[t0:, 3 * gw:] + jnp.where(pos >= 8, l3[t0 - 8:t0 - 8 + tile, 3 * gw:], 0.0))
    for g, w in enumerate(POOL_WINDOWS):
        cnt = jnp.minimum(pos + 1, w).astype(F32)
        o_ref[:, g * gw:(g + 1) * gw] = (sums[g] / cnt - x[:, g * gw:(g + 1) * gw]).astype(BF16)


def _pool_prompt(x, batch, seq, tile):
    n, d = x.shape
    hb = tile // HALO
    return pl.pallas_call(
        functools.partial(_pool_prompt_kernel, tile=tile, seq=seq),
        grid=(batch * seq // tile,),
        in_specs=[pl.BlockSpec((tile, d), lambda i: (i, 0)),
                  pl.BlockSpec((HALO, d), lambda i: (jnp.maximum(i * hb - 1, 0), 0))],
        out_specs=pl.BlockSpec((tile, d), lambda i: (i, 0)),
        out_shape=jax.ShapeDtypeStruct((n, d), BF16),
        scratch_shapes=[pltpu.VMEM((tile + 2 * HALO, d), F32)] * 4,
        compiler_params=_params("parallel"),
        name="pool_prompt",
    )(x, x)


def _pool_sample_kernel(st_ref, x_ref, pin_ref, o_ref):
    del pin_ref
    d = x_ref.shape[1]
    gw = d // len(POOL_WINDOWS)
    x = x_ref[...]
    for g, w in enumerate(POOL_WINDOWS):
        cs = slice(g * gw, (g + 1) * gw)
        acc = x[:, cs]
        for r in range(POOL_CTX - (w - 1), POOL_CTX):
            acc = acc + st_ref[r, :, cs]
        o_ref[:, cs] = (acc / float(w) - x[:, cs]).astype(BF16)


def _pool_sample(state_t, x, pooled, db):
    n, d = x.shape
    blk = n // db - 1
    return pl.pallas_call(
        _pool_sample_kernel,
        grid=(1,),
        in_specs=[pl.BlockSpec(state_t.shape, lambda i: (0, 0, 0)),
                  pl.BlockSpec((db, d), lambda i: (blk, 0)),
                  pl.BlockSpec(memory_space=pl.ANY)],
        out_specs=pl.BlockSpec((db, d), lambda i: (blk, 0)),
        out_shape=jax.ShapeDtypeStruct(pooled.shape, pooled.dtype),
        input_output_aliases={2: 0},
        compiler_params=_params("arbitrary"),
        name="pool_sample",
    )(state_t, x, pooled)


def _route_rows(logits_t, bias_ref):
    s = 1.0 / (1.0 + jnp.exp(-logits_t))
    sr = [s[e:e + 1, :] for e in range(N_EXPERTS)]
    br = [sr[e] + bias_ref[e:e + 1, :] for e in range(N_EXPERTS)]
    gscore = []
    for g in range(N_EXPERT_GROUPS):
        r = br[g * EXPERTS_PER_GROUP:(g + 1) * EXPERTS_PER_GROUP]
        best2 = None
        for a in range(EXPERTS_PER_GROUP):
            for b in range(a + 1, EXPERTS_PER_GROUP):
                pair = r[a] + r[b]
                best2 = pair if best2 is None else jnp.maximum(best2, pair)
        gscore.append(best2)
    top = functools.reduce(jnp.maximum, gscore)
    best = jnp.full(top.shape, N_EXPERT_GROUPS - 1, jnp.int32)
    for g in range(N_EXPERT_GROUPS - 2, -1, -1):
        best = jnp.where(gscore[g] == top, g, best)
    sel = []
    for e in range(N_EXPERTS):
        g = e // EXPERTS_PER_GROUP
        ahead = jnp.zeros(top.shape, F32)
        for o in range(g * EXPERTS_PER_GROUP, (g + 1) * EXPERTS_PER_GROUP):
            if o == e:
                continue
            beats = (br[o] >= br[e]) if o < e else (br[o] > br[e])
            ahead = ahead + jnp.where(beats, 1.0, 0.0)
        sel.append(jnp.where(ahead < float(TOP_K), 1.0, 0.0) * jnp.where(best == g, 1.0, 0.0))
    den = functools.reduce(lambda a, b: a + b, [sel[e] * sr[e] for e in range(N_EXPERTS)])
    wts = [sel[e] * sr[e] / den for e in range(N_EXPERTS)]
    return sel, wts


def _post_mix_kernel(a_ref, x_ref, w_ref, sc_ref, g_ref, b_ref, wrh_ref, wrl_ref, rb_ref, tri_ref,
                     xe_ref, cls_ref, rank_ref, cnt_ref, cnt_scr, *, alpha):
    i = pl.program_id(0)
    d = x_ref.shape[1]

    @pl.when(i == 0)
    def _():
        cnt_scr[...] = jnp.zeros(cnt_scr.shape, F32)

    mix = jnp.dot(a_ref[...], w_ref[...], preferred_element_type=F32) * sc_ref[...]
    x1 = _layer_norm(alpha * x_ref[...] + mix, g_ref[...], b_ref[...])
    _store_chunks(xe_ref, x1, EXT_ROWS)
    x_hi = x1.astype(BF16)
    x_lo = (x1 - x_hi.astype(F32)).astype(BF16)
    wrh = wrh_ref[...]
    logits_t = (lax.dot_general(wrh, x_hi, NT_DIMS, preferred_element_type=F32)
                + lax.dot_general(wrh, x_lo, NT_DIMS, preferred_element_type=F32)
                + lax.dot_general(wrl_ref[...], x_hi, NT_DIMS, preferred_element_type=F32))
    sel, wts = _route_rows(logits_t, rb_ref)
    t = logits_t.shape[1]
    ind = []
    w_a = jnp.zeros(sel[0].shape, F32)
    w_b = jnp.zeros(sel[0].shape, F32)
    cls = jnp.zeros(sel[0].shape, F32)
    for g in range(N_EXPERT_GROUPS):
        for k in range(N_PAIRS):
            ea = g * EXPERTS_PER_GROUP + PAIR_A[k]
            eb = g * EXPERTS_PER_GROUP + PAIR_B[k]
            hit = sel[ea] * sel[eb]
            w_a = w_a + hit * wts[ea]
            w_b = w_b + hit * wts[eb]
            cls = cls + hit * float(len(ind))
            ind.append(hit)
    crow = lax.broadcasted_iota(jnp.int32, (N_CLASSES, t), 0)
    ind_t = jnp.zeros((N_CLASSES, t), F32)
    for c in range(N_CLASSES):
        ind_t = jnp.where(crow == c, ind[c], ind_t)
    before = jnp.dot(ind_t.astype(BF16), tri_ref[...], preferred_element_type=F32)
    rank_t = cnt_scr[...] + before
    cnt_new = cnt_scr[...] + jnp.sum(ind_t, -1, keepdims=True)
    cnt_scr[...] = cnt_new
    cnt_ref[...] = cnt_new
    rank = jnp.zeros(sel[0].shape, F32)
    for c in range(N_CLASSES):
        rank = rank + ind[c] * rank_t[c:c + 1, :]
    cls_ref[...] = cls.astype(jnp.int32)
    rank_ref[...] = rank.astype(jnp.int32)
    lrow = lax.broadcasted_iota(jnp.int32, (LANES, t), 0)
    w_t = jnp.where(lrow == 0, w_a, jnp.where(lrow == 1, w_b, 0.0))
    n_chunks = d // LANES
    xe_ref[pl.ds(n_chunks, t, stride=EXT_ROWS), :] = w_t.T
    for c in range(n_chunks + 1, EXT_ROWS):
        xe_ref[pl.ds(c, t, stride=EXT_ROWS), :] = jnp.zeros((t, LANES), F32)


def _post_mix(a, x, w, scale, g, b, wr_hi, wr_lo, rbias, tri, alpha, tm):
    n, d = x.shape
    full = lambda t: pl.BlockSpec(t.shape, lambda i: (0,) * t.ndim)
    rows = lambda wd: pl.BlockSpec((tm, wd), lambda i: (i, 0))
    cols = pl.BlockSpec((1, tm), lambda i: (0, i))
    return pl.pallas_call(
        functools.partial(_post_mix_kernel, alpha=alpha),
        grid=(n // tm,),
        in_specs=[rows(d), rows(d), full(w), full(scale), full(g), full(b), full(wr_hi), full(wr_lo),
                  full(rbias), full(tri)],
        out_specs=[pl.BlockSpec((tm * EXT_ROWS, LANES), lambda i: (i, 0)), cols, cols,
                   pl.BlockSpec((N_CLASSES, 1), lambda i: (0, 0))],
        out_shape=[jax.ShapeDtypeStruct((n * EXT_ROWS, LANES), F32),
                   jax.ShapeDtypeStruct((1, n), jnp.int32), jax.ShapeDtypeStruct((1, n), jnp.int32),
                   jax.ShapeDtypeStruct((N_CLASSES, 1), F32)],
        scratch_shapes=[pltpu.VMEM((N_CLASSES, 1), F32)],
        compiler_params=_params("arbitrary"),
        name="post_mix",
    )(a, x, w, scale, g, b, wr_hi, wr_lo, rbias, tri)


def _token_copy(src_ref, src_tok, dst_ref, dst_tok, rows, sem):
    src = src_ref.at[pl.ds(pl.multiple_of(src_tok * rows, rows), rows), :]
    dst = dst_ref.at[pl.ds(pl.multiple_of(dst_tok * rows, rows), rows), :]
    return pltpu.make_async_copy(src, dst, sem)


def _wait_tokens(src_ref, dst_ref, tokens, rows, sem):
    span = pl.ds(0, tokens * rows)
    pltpu.make_async_copy(src_ref.at[span, :], dst_ref.at[span, :], sem).wait()


def _dispatch_kernel(pos_ref, x_ref, xs_in_ref, xs_ref, sem, *, tm):
    del xs_in_ref
    i = pl.program_id(0)
    base = i * tm

    def start(r, carry):
        _token_copy(x_ref, base + r, xs_ref, pos_ref[base + r], EXT_ROWS, sem).start()
        return carry

    lax.fori_loop(0, tm, start, 0, unroll=8)

    @pl.when(i > 0)
    def _():
        _wait_tokens(x_ref, xs_ref, tm, EXT_ROWS, sem)

    @pl.when(i == pl.num_programs(0) - 1)
    def _():
        _wait_tokens(x_ref, xs_ref, tm, EXT_ROWS, sem)


def _dispatch(pos, xe, xs_init, tm):
    n = xe.shape[0] // EXT_ROWS
    grid_spec = pltpu.PrefetchScalarGridSpec(
        num_scalar_prefetch=1,
        grid=(n // tm,),
        in_specs=[pl.BlockSpec(memory_space=pl.ANY), pl.BlockSpec(memory_space=pl.ANY)],
        out_specs=pl.BlockSpec(memory_space=pl.ANY),
        scratch_shapes=[pltpu.SemaphoreType.DMA(())],
    )
    return pl.pallas_call(
        functools.partial(_dispatch_kernel, tm=tm),
        grid_spec=grid_spec,
        out_shape=jax.ShapeDtypeStruct(xs_init.shape, xs_init.dtype),
        input_output_aliases={2: 0},
        compiler_params=_params("arbitrary"),
        name="moe_dispatch",
    )(pos, xe, xs_init)


def _unpermute_kernel(pos_ref, xs_ref, o_ref, buf, sem, *, tm):
    i = pl.program_id(0)
    rows = o_ref.shape[1] // LANES

    def issue(tile, slot):
        base = tile * tm

        def pair(r2, carry):
            for u in range(2):
                r = 2 * r2 + u
                _token_copy(xs_ref, pos_ref[base + r], buf.at[slot], r, rows, sem.at[slot]).start(priority=u)
            return carry

        lax.fori_loop(0, tm // 2, pair, 0, unroll=4)

    @pl.when(i == 0)
    def _():
        issue(0, 0)

    @pl.when(i + 1 < pl.num_programs(0))
    def _():
        issue(i + 1, (i + 1) % 2)

    slot = i % 2
    _wait_tokens(xs_ref, buf.at[slot], tm, rows, sem.at[slot])
    o_ref[...] = _load_chunks(buf.at[slot], tm, rows, rows)


def _unpermute(pos, xs, n, d, tm):
    rows = d // LANES
    assert tm % 2 == 0
    grid_spec = pltpu.PrefetchScalarGridSpec(
        num_scalar_prefetch=1,
        grid=(n // tm,),
        in_specs=[pl.BlockSpec(memory_space=pl.ANY)],
        out_specs=pl.BlockSpec((tm, d), lambda i, p: (i, 0)),
        scratch_shapes=[pltpu.VMEM((2, tm * rows, LANES), xs.dtype), pltpu.SemaphoreType.DMA((2,))],
    )
    return pl.pallas_call(
        functools.partial(_unpermute_kernel, tm=tm),
        grid_spec=grid_spec,
        out_shape=jax.ShapeDtypeStruct((n, d), xs.dtype),
        compiler_params=_params("arbitrary"),
        name="moe_unpermute",
    )(pos, xs)


def _moe_kernel(ta_ref, tb_ref, nu_ref, x_ref, wga_ref, wua_ref, wda_ref, wgb_ref, wub_ref, wdb_ref,
                g_ref, b_ref, o_ref, wga_s, wua_s, wda_s, wgb_s, wub_s, wdb_s, *, alpha, tile):
    i = pl.program_id(0)
    n_chunks = o_ref.shape[0] // tile

    @pl.when(i < nu_ref[0])
    def _():
        prev = jnp.maximum(i - 1, 0)

        @pl.when((i == 0) | (ta_ref[i] != ta_ref[prev]))
        def _():
            wga_s[...] = wga_ref[...].astype(BF16)
            wua_s[...] = wua_ref[...].astype(BF16)
            wda_s[...] = wda_ref[...].astype(BF16)

        @pl.when((i == 0) | (tb_ref[i] != tb_ref[prev]))
        def _():
            wgb_s[...] = wgb_ref[...].astype(BF16)
            wub_s[...] = wub_ref[...].astype(BF16)
            wdb_s[...] = wdb_ref[...].astype(BF16)

        x = _load_chunks(x_ref, tile, EXT_ROWS, n_chunks)
        combine = x_ref[pl.ds(n_chunks, tile, stride=EXT_ROWS), :]
        xb = x.astype(BF16)
        y = None
        for col, (wg, wu, wd) in enumerate(((wga_s, wua_s, wda_s), (wgb_s, wub_s, wdb_s))):
            gate = jnp.dot(xb, wg[...], preferred_element_type=F32)
            up = jnp.dot(xb, wu[...], preferred_element_type=F32)
            h = gate * (1.0 / (1.0 + jnp.exp(-gate))) * up * combine[:, col:col + 1]
            part = jnp.dot(h.astype(BF16), wd[...], preferred_element_type=F32)
            y = part if y is None else y + part
        _store_chunks(o_ref, _layer_norm(alpha * x + y, g_ref[...], b_ref[...]), n_chunks)


def _moe(tile_a, tile_b, n_used, xs, w_gate, w_up, w_down, layer, g, b, alpha, tile):
    p = xs.shape[0] // EXT_ROWS
    d, f = w_gate.shape[-2:]
    ch = d // LANES
    row_blk = lambda i, ta, tb, nu: (jnp.minimum(i, nu[0] - 1), 0)
    wa = lambda r, c: pl.BlockSpec((None, None, r, c), lambda i, ta, tb, nu: (layer, ta[i], 0, 0))
    wb = lambda r, c: pl.BlockSpec((None, None, r, c), lambda i, ta, tb, nu: (layer, tb[i], 0, 0))
    full = lambda t: pl.BlockSpec(t.shape, lambda i, ta, tb, nu: (0,) * t.ndim)
    grid_spec = pltpu.PrefetchScalarGridSpec(
        num_scalar_prefetch=3,
        grid=(p // tile,),
        in_specs=[pl.BlockSpec((tile * EXT_ROWS, LANES), row_blk), wa(d, f), wa(d, f), wa(f, d),
                  wb(d, f), wb(d, f), wb(f, d), full(g), full(b)],
        out_specs=pl.BlockSpec((tile * ch, LANES), row_blk),
        scratch_shapes=[pltpu.VMEM((d, f), BF16), pltpu.VMEM((d, f), BF16), pltpu.VMEM((f, d), BF16)] * 2,
    )
    return pl.pallas_call(
        functools.partial(_moe_kernel, alpha=alpha, tile=tile),
        grid_spec=grid_spec,
        out_shape=jax.ShapeDtypeStruct((p * ch, LANES), F32),
        compiler_params=_params("arbitrary"),
        name="moe_grouped",
    )(tile_a, tile_b, n_used, xs, w_gate, w_up, w_down, w_gate, w_up, w_down, g, b)


def _dispatch_tables(cls, rank, counts, n_tiles):
    padded = ((counts + MOE_TILE - 1) // MOE_TILE) * MOE_TILE
    ends = jnp.cumsum(padded)
    pos = (ends - padded)[cls] + rank
    tile_start = jnp.arange(n_tiles, dtype=jnp.int32) * MOE_TILE
    tile_cls = jnp.minimum(jnp.sum((tile_start[:, None] >= ends[None, :]).astype(jnp.int32), 1), N_CLASSES - 1)
    group0 = (tile_cls // N_PAIRS) * EXPERTS_PER_GROUP
    tile_a = group0 + jnp.asarray(PAIR_A, jnp.int32)[tile_cls % N_PAIRS]
    tile_b = group0 + jnp.asarray(PAIR_B, jnp.int32)[tile_cls % N_PAIRS]
    n_used = (ends[-1:] // MOE_TILE).astype(jnp.int32)
    return pos.astype(jnp.int32), tile_a.astype(jnp.int32), tile_b.astype(jnp.int32), n_used


def _rope_tables(pos):
    inv = ROPE_THETA ** (-2.0 * jnp.arange(HALF_ROPE, dtype=F32) / QK_ROPE)
    ang = pos.astype(F32)[:, None] * inv[None, :]
    cos, sin = jnp.cos(ang), jnp.sin(ang)
    n = pos.shape[0]
    ones = jnp.ones((n, ROPE_LANE0), F32)
    zeros = jnp.zeros((n, ROPE_LANE0), F32)
    tail1 = jnp.ones((n, LANES - ROPE_LANE0 - QK_ROPE), F32)
    tail0 = jnp.zeros((n, LANES - ROPE_LANE0 - QK_ROPE), F32)
    zh = jnp.zeros((n, HALF_ROPE), F32)
    cos_t = jnp.concatenate([ones, cos, cos, tail1], 1)
    sin_up = jnp.concatenate([zeros, zh, sin, tail0], 1)
    sin_dn = jnp.concatenate([zeros, -sin, zh, tail0], 1)
    return cos_t, sin_up, sin_dn


def _pad_heads(w, per_head):
    k = w.shape[0]
    w = w.reshape(k, N_HEADS, per_head)
    return jnp.pad(w, ((0, 0), (0, 0), (0, LANES - per_head))).reshape(k, N_HEADS * LANES)


def kernel(x_prompt, x_sample, cache_ckv, cache_krope, state_pool, page_table, w_dq, g_q, w_uq, w_dkv, g_kv,
           w_uk, w_uv, w_o, w_pool, pool_scale, w_router, router_bias, w_gate, w_up, w_down,
           ln1_g, ln1_b, ln2_g, ln2_b):
    batch, seq, d = x_prompt.shape
    db, dec_t, _ = x_sample.shape
    assert dec_t == 1 and d % LANES == 0 and seq % ATTN_TILE == 0 and seq % POOL_TILE == 0
    depth = ln1_g.shape[0]
    alpha = (2 * depth) ** 0.25
    n_prompt = batch * seq
    n = n_prompt + db
    assert n % db == 0 and db % 8 == 0
    tm = 384 if n % 384 == 0 else 128
    assert n % tm == 0
    n_pages = page_table.shape[1]
    past_len = n_pages * cache_ckv.shape[2]
    pps = min(PAGES_PER_STEP, n_pages)
    assert n_pages % pps == 0
    n_moe_tiles = (n + N_CLASSES * (MOE_TILE - 1)) // MOE_TILE
    row = lambda v: v.reshape(1, -1).astype(F32)

    pos = jnp.concatenate([jnp.tile(jnp.arange(seq), batch), jnp.full((db,), past_len)])
    cos_t, sin_up, sin_dn = _rope_tables(pos)
    tri = (jnp.arange(tm)[:, None] < jnp.arange(tm)[None, :]).astype(BF16)
    wr_t = w_router.T.astype(F32)
    wr_hi = wr_t.astype(BF16)
    wr_lo = (wr_t - wr_hi.astype(F32)).astype(BF16)
    rbias = router_bias.reshape(N_EXPERTS, 1).astype(F32)
    ones_d = jnp.ones((1, d), F32)
    cache_krt = jnp.swapaxes(cache_krope, 2, 3)
    assert d // LANES < EXT_ROWS
    xs = jnp.zeros((n_moe_tiles * MOE_TILE * EXT_ROWS, LANES), F32)

    x = jnp.concatenate([x_prompt.reshape(n_prompt, d), x_sample.reshape(db, d)], 0)
    ckv_p, kr_p, pool_p, ckv_s, kr_s, pool_s = [], [], [], [], [], []
    for i in range(depth):
        j = i // 2
        if i % 2 == 0:
            wuq_p = _pad_heads(w_uq[j], QK_NOPE + QK_ROPE).astype(BF16)
            wdkv_p = jnp.concatenate(
                [w_dkv[j][:, :KV_LORA], jnp.zeros((d, ROPE_LANE0), F32), w_dkv[j][:, KV_LORA:],
                 jnp.zeros((d, LANES - ROPE_LANE0 - QK_ROPE), F32)], 1).astype(BF16)
            wuk_p = _pad_heads(w_uk[j].reshape(KV_LORA, N_HEADS * QK_NOPE), QK_NOPE).astype(BF16)
            wuv_t = w_uv[j].reshape(KV_LORA, N_HEADS * V_HEAD).T.astype(BF16)
            q_all, k_all, vt_all, ckv_all, kr_all = _mla_proj(
                x, w_dq[j].astype(BF16), row(g_q[j]), wuq_p, wdkv_p, row(g_kv[j]), wuk_p, wuv_t,
                cos_t, sin_up, sin_dn, tm)
            kr_all = kr_all[:, ROPE_LANE0:ROPE_LANE0 + QK_ROPE]
            o_all = _flash_attention(q_all, k_all, vt_all, batch, seq, ATTN_TILE)
            w_ukt = jnp.pad(jnp.transpose(w_uk[j], (1, 2, 0)), ((0, 0), (0, LANES - QK_NOPE), (0, 0))).astype(BF16)
            q_lat = jnp.transpose(_q_latent(q_all, w_ukt, db), (1, 0, 2))
            q_rope = q_all[n_prompt:].reshape(db, N_HEADS, LANES)[:, :, ROPE_LANE0:ROPE_LANE0 + QK_ROPE]
            o_lat = _paged_attention(page_table, q_lat, q_rope, ckv_all[n_prompt:].reshape(db, 1, KV_LORA),
                                     kr_all[n_prompt:].reshape(db, 1, QK_ROPE), cache_ckv, cache_krt, j, pps)
            wv = jnp.transpose(w_uv[j], (1, 0, 2))
            wv_even = jnp.pad(wv, ((0, 0), (0, 0), (0, V_HEAD)))
            wv_odd = jnp.pad(wv, ((0, 0), (0, 0), (V_HEAD, 0)))
            wv_pairs = jnp.where((jnp.arange(N_HEADS) % 2 == 0)[:, None, None], wv_even, wv_odd).astype(BF16)
            mixed = _sample_out(jnp.transpose(o_lat, (1, 0, 2)), wv_pairs, o_all, db)
            w_mix, mix_scale = w_o[j].astype(BF16), ones_d
            ckv_p.append(ckv_all[:n_prompt].reshape(batch, seq, KV_LORA))
            kr_p.append(kr_all[:n_prompt].reshape(batch, seq, QK_ROPE))
            ckv_s.append(ckv_all[n_prompt:].reshape(db, 1, KV_LORA))
            kr_s.append(kr_all[n_prompt:].reshape(db, 1, QK_ROPE))
        else:
            pooled = _pool_prompt(x, batch, seq, POOL_TILE)
            mixed = _pool_sample(jnp.transpose(state_pool[j], (1, 0, 2)).astype(F32), x, pooled, db)
            w_mix = jax.scipy.linalg.block_diag(*[w_pool[j][g] for g in range(len(POOL_WINDOWS))]).astype(BF16)
            mix_scale = row(pool_scale[j])
            pool_p.append(x[:n_prompt].reshape(batch, seq, d)[:, seq - POOL_CTX:])
            pool_s.append(jnp.concatenate([state_pool[j][:, 1:].astype(F32), x[n_prompt:, None, :]], 1))
        xe, cls, rank, counts = _post_mix(
            mixed, x, w_mix, mix_scale, row(ln1_g[i]), row(ln1_b[i]), wr_hi, wr_lo, rbias, tri, alpha, tm)
        row_pos, tile_a, tile_b, n_used = _dispatch_tables(
            cls.reshape(-1), rank.reshape(-1), counts.reshape(-1).astype(jnp.int32), n_moe_tiles)
        xs = _dispatch(row_pos, xe, xs, tm)
        ys = _moe(tile_a, tile_b, n_used, xs, w_gate, w_up, w_down, i,
                  row(ln2_g[i]), row(ln2_b[i]), alpha, MOE_TILE)
        x = _unpermute(row_pos, ys, n, d, tm)
    return (x[:n_prompt].reshape(batch, seq, d), x[n_prompt:].reshape(db, 1, d),
            jnp.stack(ckv_p), jnp.stack(kr_p), jnp.stack(pool_p),
            jnp.stack(ckv_s), jnp.stack(kr_s), jnp.stack(pool_s))
```

```python
import functools
import math

import jax
import jax.numpy as jnp
from jax import lax
from jax.experimental import pallas as pl
from jax.experimental.pallas import tpu as pltpu

F32 = jnp.float32
BF16 = jnp.bfloat16

N_HEADS = 16
QK_NOPE = 64
QK_ROPE = 32
V_HEAD = 64
KV_LORA = 256
ROPE_THETA = 10000.0
ATTN_SCALE = (QK_NOPE + QK_ROPE) ** -0.5
POOL_WINDOWS = (2, 4, 8, 16)
POOL_CTX = max(POOL_WINDOWS) - 1
N_EXPERTS = 16
N_EXPERT_GROUPS = 4
EXPERTS_PER_GROUP = N_EXPERTS // N_EXPERT_GROUPS
TOP_K = 2
LN_EPS = 1e-5
RMS_EPS = 1e-6

PAIR_A = (0, 0, 0, 1, 1, 3)
PAIR_B = (1, 2, 3, 3, 2, 2)
N_PAIRS = len(PAIR_A)
N_CLASSES = N_EXPERT_GROUPS * N_PAIRS

LANES = 128
HALF_ROPE = QK_ROPE // 2
ROPE_LANE0 = QK_NOPE
HALO = 16
NEG_BIG = -1e30
LOG2E = math.log2(math.e)
VMEM_LIMIT = 52 * 1024 * 1024
MOE_TILE = 256
ATTN_TILE = 512
ATTN_HEADS_PER_STEP = 4
ONES_ROWS = 16
POOL_TILE = 512
PAGES_PER_STEP = 16
SUBLANES = 8
EXT_ROWS = 2 * SUBLANES
EXT_USED = SUBLANES + 1
NT_DIMS = (((1,), (1,)), ((), ()))


def _params(*sem):
    return pltpu.CompilerParams(dimension_semantics=sem, vmem_limit_bytes=VMEM_LIMIT)


def _rms(x, g):
    return x * lax.rsqrt(jnp.mean(x * x, -1, keepdims=True) + RMS_EPS) * g


def _layer_norm(y, g, b):
    mu = jnp.mean(y, -1, keepdims=True)
    d = y - mu
    var = jnp.mean(d * d, -1, keepdims=True)
    return d * lax.rsqrt(var + LN_EPS) * g + b


def _load_chunks(ref, tokens, rows, n_chunks):
    return jnp.concatenate([ref[pl.ds(c, tokens, stride=rows), :] for c in range(n_chunks)], axis=1)


def _store_chunks(ref, val, rows):
    tokens, width = val.shape
    for c in range(width // LANES):
        ref[pl.ds(c, tokens, stride=rows), :] = val[:, c * LANES:(c + 1) * LANES]


def _rope_chunk(c, cos, sin_up, sin_dn):
    return c * cos + pltpu.roll(c, HALF_ROPE, 1) * sin_up + pltpu.roll(c, LANES - HALF_ROPE, 1) * sin_dn


def _mla_proj_kernel(x_ref, wdq_ref, gq_ref, wuq_ref, wdkv_ref, gkv_ref, wuk_ref, wuvt_ref,
                     cos_ref, sup_ref, sdn_ref, q_ref, k_ref, vt_ref, ckv_ref, kr_ref):
    xb = x_ref[...].astype(BF16)
    cos, sup, sdn = cos_ref[...], sup_ref[...], sdn_ref[...]
    cq = _rms(jnp.dot(xb, wdq_ref[...], preferred_element_type=F32), gq_ref[...])
    q = jnp.dot(cq.astype(BF16), wuq_ref[...], preferred_element_type=F32)
    kv = jnp.dot(xb, wdkv_ref[...], preferred_element_type=F32)
    ckv = _rms(kv[:, :KV_LORA], gkv_ref[...])
    ckv_ref[...] = ckv
    kr = _rope_chunk(kv[:, KV_LORA:], cos, sup, sdn)
    kr_ref[...] = kr
    ckv_b = ckv.astype(BF16)
    kn = jnp.dot(ckv_b, wuk_ref[...], preferred_element_type=F32)
    vt_ref[...] = lax.dot_general(wuvt_ref[...], ckv_b, NT_DIMS, preferred_element_type=F32).astype(BF16)
    for h in range(N_HEADS):
        sl = slice(h * LANES, (h + 1) * LANES)
        q_ref[:, sl] = (_rope_chunk(q[:, sl], cos, sup, sdn) * (ATTN_SCALE * LOG2E)).astype(BF16)
        k_ref[:, sl] = (kn[:, sl] + kr).astype(BF16)


def _mla_proj(x, wdq, gq, wuq, wdkv, gkv, wuk, wuvt, cos, sup, sdn, tm):
    n, d = x.shape
    full = lambda a: pl.BlockSpec(a.shape, lambda i: (0,) * a.ndim)
    rows = lambda w: pl.BlockSpec((tm, w), lambda i: (i, 0))
    hp = N_HEADS * LANES
    hv = N_HEADS * V_HEAD
    return pl.pallas_call(
        _mla_proj_kernel,
        grid=(n // tm,),
        in_specs=[rows(d), full(wdq), full(gq), full(wuq), full(wdkv), full(gkv), full(wuk), full(wuvt),
                  rows(LANES), rows(LANES), rows(LANES)],
        out_specs=[rows(hp), rows(hp), pl.BlockSpec((hv, tm), lambda i: (0, i)), rows(KV_LORA), rows(LANES)],
        out_shape=[jax.ShapeDtypeStruct((n, hp), BF16), jax.ShapeDtypeStruct((n, hp), BF16),
                   jax.ShapeDtypeStruct((hv, n), BF16),
                   jax.ShapeDtypeStruct((n, KV_LORA), F32), jax.ShapeDtypeStruct((n, LANES), F32)],
        compiler_params=_params("parallel"),
        name="mla_proj",
    )(x, wdq, gq, wuq, wdkv, gkv, wuk, wuvt, cos, sup, sdn)


def _flash_kernel(q_ref, k_ref, vt_ref, o_ref, *, tile, heads):
    qi = pl.program_id(2)
    krow = lax.broadcasted_iota(jnp.int32, (tile, tile), 0)
    qcol = lax.broadcasted_iota(jnp.int32, (tile, tile), 1)
    ones = jnp.ones((ONES_ROWS, tile), BF16)

    def step(kv, carry, masked):
        start = pl.multiple_of(kv * tile, tile)
        new = []
        scores = [lax.dot_general(k_ref[pl.ds(start, tile), hh * LANES:(hh + 1) * LANES],
                                  q_ref[:, hh * LANES:(hh + 1) * LANES], NT_DIMS, preferred_element_type=F32)
                  for hh in range(heads)]
        for hh in range(heads):
            m, acc = carry[hh]
            s = scores[hh]
            if masked:
                s = jnp.where(krow <= qcol, s, NEG_BIG)
            m_new = jnp.maximum(m, jnp.max(s, 0, keepdims=True))
            p = jnp.exp2(s - m_new).astype(BF16)
            a = jnp.exp2(m - m_new)
            vt = jnp.concatenate([vt_ref[hh * V_HEAD:(hh + 1) * V_HEAD, pl.ds(start, tile)], ones], axis=0)
            acc = a * acc + jnp.dot(vt, p, preferred_element_type=F32)
            new.append((m_new, acc))
        return tuple(new)

    init = tuple((jnp.full((1, tile), NEG_BIG, F32), jnp.zeros((V_HEAD + ONES_ROWS, tile), F32))
                 for _ in range(heads))
    carry = lax.fori_loop(0, qi, lambda kv, c: step(kv, c, False), init)
    carry = step(qi, carry, True)
    o_t = jnp.concatenate([acc[:V_HEAD] / acc[V_HEAD:V_HEAD + 1] for _, acc in carry], axis=0)
    o_ref[...] = o_t.T.astype(BF16)


def _flash_attention(q, k, vt, batch, seq, tile):
    n = q.shape[0]
    nq = seq // tile
    hps = ATTN_HEADS_PER_STEP
    return pl.pallas_call(
        functools.partial(_flash_kernel, tile=tile, heads=hps),
        grid=(batch, N_HEADS // hps, nq),
        in_specs=[pl.BlockSpec((tile, hps * LANES), lambda b, p, i: (b * nq + i, p)),
                  pl.BlockSpec((seq, hps * LANES), lambda b, p, i: (b, p)),
                  pl.BlockSpec((hps * V_HEAD, seq), lambda b, p, i: (p, b))],
        out_specs=pl.BlockSpec((tile, hps * V_HEAD), lambda b, p, i: (b * nq + i, p)),
        out_shape=jax.ShapeDtypeStruct((n, N_HEADS * V_HEAD), BF16),
        compiler_params=_params("parallel", "parallel", "arbitrary"),
        name="flash_prompt",
    )(q, k, vt)


def _qlat_kernel(q_ref, w_ref, o_ref):
    o_ref[...] = jnp.dot(q_ref[...], w_ref[...], preferred_element_type=F32).astype(BF16)


def _q_latent(q, w_ukt, db):
    n = q.shape[0]
    blk = n // db - 1
    return pl.pallas_call(
        _qlat_kernel,
        grid=(N_HEADS,),
        in_specs=[pl.BlockSpec((db, LANES), lambda h: (blk, h)),
                  pl.BlockSpec((None, LANES, KV_LORA), lambda h: (h, 0, 0))],
        out_specs=pl.BlockSpec((None, db, KV_LORA), lambda h: (h, 0, 0)),
        out_shape=jax.ShapeDtypeStruct((N_HEADS, db, KV_LORA), BF16),
        compiler_params=_params("parallel"),
        name="q_latent",
    )(q, w_ukt)


def _paged_kernel(pt_ref, ql_ref, qr_ref, cnew_ref, knew_ref, ckv_hbm, krt_hbm, o_ref, kbuf, rbuf, sem,
                  *, layer, n_pages, psz, pps):
    b = pl.program_id(0)

    def page_copies(seq, slot):
        for i in range(n_pages):
            page = pt_ref[seq * n_pages + i]
            yield pltpu.make_async_copy(ckv_hbm.at[layer, page], kbuf.at[slot, pl.ds(i * psz, psz), :],
                                        sem.at[slot])
            yield pltpu.make_async_copy(krt_hbm.at[layer, page], rbuf.at[slot, :, pl.ds(i * psz, psz)],
                                        sem.at[slot])

    @pl.when(b == 0)
    def _():
        for cp in page_copies(0, 0):
            cp.start()

    @pl.when(b + 1 < pl.num_programs(0))
    def _():
        for cp in page_copies(b + 1, (b + 1) % 2):
            cp.start()

    slot = b % 2
    for cp in page_copies(b, slot):
        cp.wait()

    ql = ql_ref[...]
    qr = qr_ref[...]
    width = pps * psz
    chunks = range(n_pages // pps)
    keys = [kbuf[slot, c * width:(c + 1) * width, :].astype(BF16) for c in chunks]
    krts = [rbuf[slot, :, c * width:(c + 1) * width].astype(BF16) for c in chunks]
    scores = [lax.dot_general(ql, keys[c], NT_DIMS, preferred_element_type=F32)
              + jnp.dot(qr, krts[c], preferred_element_type=F32) for c in chunks]
    maxes = [jnp.max(s, -1, keepdims=True) for s in scores]
    probs = [jnp.exp2(scores[c] - maxes[c]).astype(BF16) for c in chunks]
    parts = [(maxes[c], jnp.sum(probs[c].astype(F32), -1, keepdims=True),
              jnp.dot(probs[c], keys[c], preferred_element_type=F32)) for c in chunks]
    cn = cnew_ref[...].astype(BF16).astype(F32)
    kn = knew_ref[...].astype(BF16).astype(F32)
    s_self = (jnp.sum(ql.astype(F32) * cn, -1, keepdims=True) + jnp.sum(qr.astype(F32) * kn, -1, keepdims=True))
    m = functools.reduce(jnp.maximum, [pt[0] for pt in parts] + [s_self])
    p_self = jnp.exp2(s_self - m).astype(BF16).astype(F32)
    l = p_self
    acc = p_self * cn
    for m_c, l_c, acc_c in parts:
        a = jnp.exp2(m_c - m)
        l = l + a * l_c
        acc = acc + a * acc_c
    o_ref[...] = acc / l


def _paged_attention(page_table, q_lat, q_rope, c_new, k_new, cache_ckv, cache_krt, layer, pps):
    db, n_pages = page_table.shape
    psz = cache_ckv.shape[2]
    pt = page_table.reshape(-1)
    per_b = lambda r, w: pl.BlockSpec((None, r, w), lambda b, pt_ref: (b, 0, 0))
    grid_spec = pltpu.PrefetchScalarGridSpec(
        num_scalar_prefetch=1,
        grid=(db,),
        in_specs=[per_b(N_HEADS, KV_LORA), per_b(N_HEADS, QK_ROPE), per_b(1, KV_LORA), per_b(1, QK_ROPE),
                  pl.BlockSpec(memory_space=pl.ANY), pl.BlockSpec(memory_space=pl.ANY)],
        out_specs=per_b(N_HEADS, KV_LORA),
        scratch_shapes=[pltpu.VMEM((2, n_pages * psz, KV_LORA), cache_ckv.dtype),
                        pltpu.VMEM((2, QK_ROPE, n_pages * psz), cache_krt.dtype),
                        pltpu.SemaphoreType.DMA((2,))],
    )
    return pl.pallas_call(
        functools.partial(_paged_kernel, layer=layer, n_pages=n_pages, psz=psz, pps=pps),
        grid_spec=grid_spec,
        out_shape=jax.ShapeDtypeStruct((db, N_HEADS, KV_LORA), F32),
        compiler_params=_params("arbitrary"),
        name="paged_sample",
    )(pt, q_lat, q_rope, c_new, k_new, cache_ckv, cache_krt)


def _sample_out_kernel(ol_ref, w_ref, oin_ref, o_ref):
    del oin_ref
    acc = jnp.dot(ol_ref[0].astype(BF16), w_ref[0], preferred_element_type=F32)
    acc = acc + jnp.dot(ol_ref[1].astype(BF16), w_ref[1], preferred_element_type=F32)
    o_ref[...] = acc.astype(BF16)


def _sample_out(o_lat_t, w_uv_pairs, o_all, db):
    n = o_all.shape[0]
    blk = n // db - 1
    return pl.pallas_call(
        _sample_out_kernel,
        grid=(N_HEADS // 2,),
        in_specs=[pl.BlockSpec((2, db, KV_LORA), lambda p: (p, 0, 0)),
                  pl.BlockSpec((2, KV_LORA, 2 * V_HEAD), lambda p: (p, 0, 0)),
                  pl.BlockSpec(memory_space=pl.ANY)],
        out_specs=pl.BlockSpec((db, 2 * V_HEAD), lambda p: (blk, p)),
        out_shape=jax.ShapeDtypeStruct(o_all.shape, o_all.dtype),
        input_output_aliases={2: 0},
        compiler_params=_params("parallel"),
        name="sample_out",
    )(o_lat_t, w_uv_pairs, o_all)


def _pool_prompt_kernel(x_ref, halo_ref, o_ref, l0, l1, l2, l3, *, tile, seq):
    i = pl.program_id(0)
    d = x_ref.shape[1]
    gw = d // len(POOL_WINDOWS)
    ext = tile + HALO
    s0 = (i * tile) % seq
    pos_ext = s0 - HALO + lax.broadcasted_iota(jnp.int32, (ext, 1), 0)
    zeros = jnp.zeros((HALO, d), F32)
    for buf in (l0, l1, l2, l3):
        buf[0:HALO, :] = zeros
    l0[HALO:2 * HALO, :] = halo_ref[...]
    l0[2 * HALO:, :] = x_ref[...]

    def level(src, dst, k, c0):
        cur = src[HALO:HALO + ext, c0:]
        sh = src[HALO - k:HALO - k + ext, c0:]
        dst[HALO:HALO + ext, c0:] = cur + jnp.where(pos_ext >= k, sh, 0.0)

    level(l0, l1, 1, 0)
    level(l1, l2, 2, gw)
    level(l2, l3, 4, 2 * gw)
    pos = pos_ext[HALO:]
    x = x_ref[...]
    t0 = 2 * HALO
    sums = (l1[t0:, 0:gw], l2[t0:, gw:2 * gw], l3[t0:, 2 * gw:3 * gw],
            l3[t0:, 3 * gw:] + jnp.where(pos >= 8, l3[t0 - 8:t0 - 8 + tile, 3 * gw:], 0.0))
    for g, w in enumerate(POOL_WINDOWS):
        cnt = jnp.minimum(pos + 1, w).astype(F32)
        o_ref[:, g * gw:(g + 1) * gw] = (sums[g] / cnt - x[:, g * gw:(g + 1) * gw]).astype(BF16)


def _pool_prompt(x, batch, seq, tile):
    n, d = x.shape
    hb = tile // HALO
    return pl.pallas_call(
        functools.partial(_pool_prompt_kernel, tile=tile, seq=seq),
        grid=(batch * seq // tile,),
        in_specs=[pl.BlockSpec((tile, d), lambda i: (i, 0)),
                  pl.BlockSpec((HALO, d), lambda i: (jnp.maximum(i * hb - 1, 0), 0))],
        out_specs=pl.BlockSpec((tile, d), lambda i: (i, 0)),
        out_shape=jax.ShapeDtypeStruct((n, d), BF16),
        scratch_shapes=[pltpu.VMEM((tile + 2 * HALO, d), F32)] * 4,
        compiler_params=_params("parallel"),
        name="pool_prompt",
    )(x, x)


def _pool_sample_kernel(st_ref, x_ref, pin_ref, o_ref):
    del pin_ref
    d = x_ref.shape[1]
    gw = d // len(POOL_WINDOWS)
    x = x_ref[...]
    for g, w in enumerate(POOL_WINDOWS):
        cs = slice(g * gw, (g + 1) * gw)
        acc = x[:, cs]
        for r in range(POOL_CTX - (w - 1), POOL_CTX):
            acc = acc + st_ref[r, :, cs]
        o_ref[:, cs] = (acc / float(w) - x[:, cs]).astype(BF16)


def _pool_sample(state_t, x, pooled, db):
    n, d = x.shape
    blk = n // db - 1
    return pl.pallas_call(
        _pool_sample_kernel,
        grid=(1,),
        in_specs=[pl.BlockSpec(state_t.shape, lambda i: (0, 0, 0)),
                  pl.BlockSpec((db, d), lambda i: (blk, 0)),
                  pl.BlockSpec(memory_space=pl.ANY)],
        out_specs=pl.BlockSpec((db, d), lambda i: (blk, 0)),
        out_shape=jax.ShapeDtypeStruct(pooled.shape, pooled.dtype),
        input_output_aliases={2: 0},
        compiler_params=_params("arbitrary"),
        name="pool_sample",
    )(state_t, x, pooled)


def _route_rows(logits_t, bias_ref):
    s = 1.0 / (1.0 + jnp.exp(-logits_t))
    sr = [s[e:e + 1, :] for e in range(N_EXPERTS)]
    br = [sr[e] + bias_ref[e:e + 1, :] for e in range(N_EXPERTS)]
    gscore = []
    for g in range(N_EXPERT_GROUPS):
        r = br[g * EXPERTS_PER_GROUP:(g + 1) * EXPERTS_PER_GROUP]
        best2 = None
        for a in range(EXPERTS_PER_GROUP):
            for b in range(a + 1, EXPERTS_PER_GROUP):
                pair = r[a] + r[b]
                best2 = pair if best2 is None else jnp.maximum(best2, pair)
        gscore.append(best2)
    top = functools.reduce(jnp.maximum, gscore)
    best = jnp.full(top.shape, N_EXPERT_GROUPS - 1, jnp.int32)
    for g in range(N_EXPERT_GROUPS - 2, -1, -1):
        best = jnp.where(gscore[g] == top, g, best)
    sel = []
    for e in range(N_EXPERTS):
        g = e // EXPERTS_PER_GROUP
        ahead = jnp.zeros(top.shape, F32)
        for o in range(g * EXPERTS_PER_GROUP, (g + 1) * EXPERTS_PER_GROUP):
            if o == e:
                continue
            beats = (br[o] >= br[e]) if o < e else (br[o] > br[e])
            ahead = ahead + jnp.where(beats, 1.0, 0.0)
        sel.append(jnp.where(ahead < float(TOP_K), 1.0, 0.0) * jnp.where(best == g, 1.0, 0.0))
    den = functools.reduce(lambda a, b: a + b, [sel[e] * sr[e] for e in range(N_EXPERTS)])
    wts = [sel[e] * sr[e] / den for e in range(N_EXPERTS)]
    return sel, wts


def _post_mix_kernel(a_ref, x_ref, w_ref, sc_ref, g_ref, b_ref, wrh_ref, wrl_ref, rb_ref, tri_ref,
                     xe_ref, cls_ref, rank_ref, cnt_ref, cnt_scr, *, alpha):
    i = pl.program_id(0)
    d = x_ref.shape[1]

    @pl.when(i == 0)
    def _():
        cnt_scr[...] = jnp.zeros(cnt_scr.shape, F32)

    mix = jnp.dot(a_ref[...], w_ref[...], preferred_element_type=F32) * sc_ref[...]
    x1 = _layer_norm(alpha * x_ref[...] + mix, g_ref[...], b_ref[...])
    _store_chunks(xe_ref, x1, EXT_ROWS)
    x_hi = x1.astype(BF16)
    x_lo = (x1 - x_hi.astype(F32)).astype(BF16)
    wrh = wrh_ref[...]
    logits_t = (lax.dot_general(wrh, x_hi, NT_DIMS, preferred_element_type=F32)
                + lax.dot_general(wrh, x_lo, NT_DIMS, preferred_element_type=F32)
                + lax.dot_general(wrl_ref[...], x_hi, NT_DIMS, preferred_element_type=F32))
    sel, wts = _route_rows(logits_t, rb_ref)
    t = logits_t.shape[1]
    ind = []
    w_a = jnp.zeros(sel[0].shape, F32)
    w_b = jnp.zeros(sel[0].shape, F32)
    cls = jnp.zeros(sel[0].shape, F32)
    for g in range(N_EXPERT_GROUPS):
        for k in range(N_PAIRS):
            ea = g * EXPERTS_PER_GROUP + PAIR_A[k]
            eb = g * EXPERTS_PER_GROUP + PAIR_B[k]
            hit = sel[ea] * sel[eb]
            w_a = w_a + hit * wts[ea]
            w_b = w_b + hit * wts[eb]
            cls = cls + hit * float(len(ind))
            ind.append(hit)
    crow = lax.broadcasted_iota(jnp.int32, (N_CLASSES, t), 0)
    ind_t = jnp.zeros((N_CLASSES, t), F32)
    for c in range(N_CLASSES):
        ind_t = jnp.where(crow == c, ind[c], ind_t)
    before = jnp.dot(ind_t.astype(BF16), tri_ref[...], preferred_element_type=F32)
    rank_t = cnt_scr[...] + before
    cnt_new = cnt_scr[...] + jnp.sum(ind_t, -1, keepdims=True)
    cnt_scr[...] = cnt_new
    cnt_ref[...] = cnt_new
    rank = jnp.zeros(sel[0].shape, F32)
    for c in range(N_CLASSES):
        rank = rank + ind[c] * rank_t[c:c + 1, :]
    cls_ref[...] = cls.astype(jnp.int32)
    rank_ref[...] = rank.astype(jnp.int32)
    lrow = lax.broadcasted_iota(jnp.int32, (LANES, t), 0)
    w_t = jnp.where(lrow == 0, w_a, jnp.where(lrow == 1, w_b, 0.0))
    xe_ref[pl.ds(EXT_USED - 1, t, stride=EXT_ROWS), :] = w_t.T


def _post_mix(a, x, w, scale, g, b, wr_hi, wr_lo, rbias, tri, alpha, tm):
    n, d = x.shape
    full = lambda t: pl.BlockSpec(t.shape, lambda i: (0,) * t.ndim)
    rows = lambda wd: pl.BlockSpec((tm, wd), lambda i: (i, 0))
    cols = pl.BlockSpec((1, tm), lambda i: (0, i))
    return pl.pallas_call(
        functools.partial(_post_mix_kernel, alpha=alpha),
        grid=(n // tm,),
        in_specs=[rows(d), rows(d), full(w), full(scale), full(g), full(b), full(wr_hi), full(wr_lo),
                  full(rbias), full(tri)],
        out_specs=[pl.BlockSpec((tm * EXT_ROWS, LANES), lambda i: (i, 0)), cols, cols,
                   pl.BlockSpec((N_CLASSES, 1), lambda i: (0, 0))],
        out_shape=[jax.ShapeDtypeStruct((n * EXT_ROWS, LANES), F32),
                   jax.ShapeDtypeStruct((1, n), jnp.int32), jax.ShapeDtypeStruct((1, n), jnp.int32),
                   jax.ShapeDtypeStruct((N_CLASSES, 1), F32)],
        scratch_shapes=[pltpu.VMEM((N_CLASSES, 1), F32)],
        compiler_params=_params("arbitrary"),
        name="post_mix",
    )(a, x, w, scale, g, b, wr_hi, wr_lo, rbias, tri)


def _token_copy(src_ref, src_tok, dst_ref, dst_tok, rows, sem, used=None):
    used = rows if used is None else used
    src = src_ref.at[pl.ds(pl.multiple_of(src_tok * rows, rows), used), :]
    dst = dst_ref.at[pl.ds(pl.multiple_of(dst_tok * rows, rows), used), :]
    return pltpu.make_async_copy(src, dst, sem)


def _wait_tokens(src_ref, dst_ref, tokens, rows, sem):
    span = pl.ds(0, tokens * rows)
    pltpu.make_async_copy(src_ref.at[span, :], dst_ref.at[span, :], sem).wait()


def _dispatch_kernel(pos_ref, x_ref, xs_in_ref, xs_ref, sem, *, tm):
    del xs_in_ref
    base = pl.program_id(0) * tm

    def pair(r2, carry):
        for u in range(2):
            r = 2 * r2 + u
            _token_copy(x_ref, r, xs_ref, pos_ref[base + r], EXT_ROWS, sem, used=EXT_USED).start(priority=u)
        return carry

    lax.fori_loop(0, tm // 2, pair, 0, unroll=4)
    _wait_tokens(x_ref, xs_ref, tm, EXT_USED, sem)


def _dispatch(pos, xe, xs_init, tm):
    n = xe.shape[0] // EXT_ROWS
    assert tm % 2 == 0
    grid_spec = pltpu.PrefetchScalarGridSpec(
        num_scalar_prefetch=1,
        grid=(n // tm,),
        in_specs=[pl.BlockSpec((tm * EXT_ROWS, LANES), lambda i, p: (i, 0)), pl.BlockSpec(memory_space=pl.ANY)],
        out_specs=pl.BlockSpec(memory_space=pl.ANY),
        scratch_shapes=[pltpu.SemaphoreType.DMA(())],
    )
    return pl.pallas_call(
        functools.partial(_dispatch_kernel, tm=tm),
        grid_spec=grid_spec,
        out_shape=jax.ShapeDtypeStruct(xs_init.shape, xs_init.dtype),
        input_output_aliases={2: 0},
        compiler_params=_params("arbitrary"),
        name="moe_dispatch",
    )(pos, xe, xs_init)


def _unpermute_kernel(pos_ref, xs_ref, o_ref, buf, sem, *, tm):
    i = pl.program_id(0)
    rows = o_ref.shape[1] // LANES

    def issue(tile, slot):
        base = tile * tm

        def pair(r2, carry):
            for u in range(2):
                r = 2 * r2 + u
                _token_copy(xs_ref, pos_ref[base + r], buf.at[slot], r, rows, sem.at[slot]).start(priority=u)
            return carry

        lax.fori_loop(0, tm // 2, pair, 0, unroll=4)

    @pl.when(i == 0)
    def _():
        issue(0, 0)

    @pl.when(i + 1 < pl.num_programs(0))
    def _():
        issue(i + 1, (i + 1) % 2)

    slot = i % 2
    _wait_tokens(xs_ref, buf.at[slot], tm, rows, sem.at[slot])
    o_ref[...] = _load_chunks(buf.at[slot], tm, rows, rows)


def _unpermute(pos, xs, n, d, tm):
    rows = d // LANES
    assert tm % 2 == 0
    grid_spec = pltpu.PrefetchScalarGridSpec(
        num_scalar_prefetch=1,
        grid=(n // tm,),
        in_specs=[pl.BlockSpec(memory_space=pl.ANY)],
        out_specs=pl.BlockSpec((tm, d), lambda i, p: (i, 0)),
        scratch_shapes=[pltpu.VMEM((2, tm * rows, LANES), xs.dtype), pltpu.SemaphoreType.DMA((2,))],
    )
    return pl.pallas_call(
        functools.partial(_unpermute_kernel, tm=tm),
        grid_spec=grid_spec,
        out_shape=jax.ShapeDtypeStruct((n, d), xs.dtype),
        compiler_params=_params("arbitrary"),
        name="moe_unpermute",
    )(pos, xs)


def _moe_kernel(ta_ref, tb_ref, nu_ref, x_ref, wga_ref, wua_ref, wda_ref, wgb_ref, wub_ref, wdb_ref,
                g_ref, b_ref, o_ref, wga_s, wua_s, wda_s, wgb_s, wub_s, wdb_s, *, alpha, tile):
    i = pl.program_id(0)
    n_chunks = o_ref.shape[0] // tile

    @pl.when(i < nu_ref[0])
    def _():
        prev = jnp.maximum(i - 1, 0)

        @pl.when((i == 0) | (ta_ref[i] != ta_ref[prev]))
        def _():
            wga_s[...] = wga_ref[...].astype(BF16)
            wua_s[...] = wua_ref[...].astype(BF16)
            wda_s[...] = wda_ref[...].astype(BF16)

        @pl.when((i == 0) | (tb_ref[i] != tb_ref[prev]))
        def _():
            wgb_s[...] = wgb_ref[...].astype(BF16)
            wub_s[...] = wub_ref[...].astype(BF16)
            wdb_s[...] = wdb_ref[...].astype(BF16)

        x = _load_chunks(x_ref, tile, EXT_ROWS, n_chunks)
        combine = x_ref[pl.ds(n_chunks, tile, stride=EXT_ROWS), :]
        xb = x.astype(BF16)
        y = None
        for col, (wg, wu, wd) in enumerate(((wga_s, wua_s, wda_s), (wgb_s, wub_s, wdb_s))):
            gate = jnp.dot(xb, wg[...], preferred_element_type=F32)
            up = jnp.dot(xb, wu[...], preferred_element_type=F32)
            h = gate * (1.0 / (1.0 + jnp.exp(-gate))) * up * combine[:, col:col + 1]
            part = jnp.dot(h.astype(BF16), wd[...], preferred_element_type=F32)
            y = part if y is None else y + part
        _store_chunks(o_ref, _layer_norm(alpha * x + y, g_ref[...], b_ref[...]), n_chunks)


def _moe(tile_a, tile_b, n_used, xs, w_gate, w_up, w_down, layer, g, b, alpha, tile):
    p = xs.shape[0] // EXT_ROWS
    d, f = w_gate.shape[-2:]
    ch = d // LANES
    row_blk = lambda i, ta, tb, nu: (jnp.minimum(i, nu[0] - 1), 0)
    wa = lambda r, c: pl.BlockSpec((None, None, r, c), lambda i, ta, tb, nu: (layer, ta[i], 0, 0))
    wb = lambda r, c: pl.BlockSpec((None, None, r, c), lambda i, ta, tb, nu: (layer, tb[i], 0, 0))
    full = lambda t: pl.BlockSpec(t.shape, lambda i, ta, tb, nu: (0,) * t.ndim)
    grid_spec = pltpu.PrefetchScalarGridSpec(
        num_scalar_prefetch=3,
        grid=(p // tile,),
        in_specs=[pl.BlockSpec((tile * EXT_ROWS, LANES), row_blk), wa(d, f), wa(d, f), wa(f, d),
                  wb(d, f), wb(d, f), wb(f, d), full(g), full(b)],
        out_specs=pl.BlockSpec((tile * ch, LANES), row_blk),
        scratch_shapes=[pltpu.VMEM((d, f), BF16), pltpu.VMEM((d, f), BF16), pltpu.VMEM((f, d), BF16)] * 2,
    )
    return pl.pallas_call(
        functools.partial(_moe_kernel, alpha=alpha, tile=tile),
        grid_spec=grid_spec,
        out_shape=jax.ShapeDtypeStruct((p * ch, LANES), F32),
        compiler_params=_params("arbitrary"),
        name="moe_grouped",
    )(tile_a, tile_b, n_used, xs, w_gate, w_up, w_down, w_gate, w_up, w_down, g, b)


def _dispatch_tables(cls, rank, counts, n_tiles):
    padded = ((counts + MOE_TILE - 1) // MOE_TILE) * MOE_TILE
    ends = jnp.cumsum(padded)
    pos = (ends - padded)[cls] + rank
    tile_start = jnp.arange(n_tiles, dtype=jnp.int32) * MOE_TILE
    tile_cls = jnp.minimum(jnp.sum((tile_start[:, None] >= ends[None, :]).astype(jnp.int32), 1), N_CLASSES - 1)
    group0 = (tile_cls // N_PAIRS) * EXPERTS_PER_GROUP
    tile_a = group0 + jnp.asarray(PAIR_A, jnp.int32)[tile_cls % N_PAIRS]
    tile_b = group0 + jnp.asarray(PAIR_B, jnp.int32)[tile_cls % N_PAIRS]
    n_used = (ends[-1:] // MOE_TILE).astype(jnp.int32)
    return pos.astype(jnp.int32), tile_a.astype(jnp.int32), tile_b.astype(jnp.int32), n_used


def _rope_tables(pos):
    inv = ROPE_THETA ** (-2.0 * jnp.arange(HALF_ROPE, dtype=F32) / QK_ROPE)
    ang = pos.astype(F32)[:, None] * inv[None, :]
    cos, sin = jnp.cos(ang), jnp.sin(ang)
    n = pos.shape[0]
    ones = jnp.ones((n, ROPE_LANE0), F32)
    zeros = jnp.zeros((n, ROPE_LANE0), F32)
    tail1 = jnp.ones((n, LANES - ROPE_LANE0 - QK_ROPE), F32)
    tail0 = jnp.zeros((n, LANES - ROPE_LANE0 - QK_ROPE), F32)
    zh = jnp.zeros((n, HALF_ROPE), F32)
    cos_t = jnp.concatenate([ones, cos, cos, tail1], 1)
    sin_up = jnp.concatenate([zeros, zh, sin, tail0], 1)
    sin_dn = jnp.concatenate([zeros, -sin, zh, tail0], 1)
    return cos_t, sin_up, sin_dn


def _pad_heads(w, per_head):
    k = w.shape[0]
    w = w.reshape(k, N_HEADS, per_head)
    return jnp.pad(w, ((0, 0), (0, 0), (0, LANES - per_head))).reshape(k, N_HEADS * LANES)


def kernel(x_prompt, x_sample, cache_ckv, cache_krope, state_pool, page_table, w_dq, g_q, w_uq, w_dkv, g_kv,
           w_uk, w_uv, w_o, w_pool, pool_scale, w_router, router_bias, w_gate, w_up, w_down,
           ln1_g, ln1_b, ln2_g, ln2_b):
    batch, seq, d = x_prompt.shape
    db, dec_t, _ = x_sample.shape
    assert dec_t == 1 and d % LANES == 0 and seq % ATTN_TILE == 0 and seq % POOL_TILE == 0
    depth = ln1_g.shape[0]
    alpha = (2 * depth) ** 0.25
    n_prompt = batch * seq
    n = n_prompt + db
    assert n % db == 0 and db % 8 == 0
    tm = 384 if n % 384 == 0 else 128
    assert n % tm == 0
    n_pages = page_table.shape[1]
    past_len = n_pages * cache_ckv.shape[2]
    pps = min(PAGES_PER_STEP, n_pages)
    assert n_pages % pps == 0
    n_moe_tiles = (n + N_CLASSES * (MOE_TILE - 1)) // MOE_TILE
    row = lambda v: v.reshape(1, -1).astype(F32)

    tables = _rope_tables(jnp.concatenate([jnp.arange(seq), jnp.full((1,), past_len)]))
    cos_t, sin_up, sin_dn = [jnp.concatenate([jnp.tile(t[:seq], (batch, 1)), jnp.tile(t[seq:], (db, 1))])
                             for t in tables]
    tri = (jnp.arange(tm)[:, None] < jnp.arange(tm)[None, :]).astype(BF16)
    wr_t = w_router.T.astype(F32)
    wr_hi = wr_t.astype(BF16)
    wr_lo = (wr_t - wr_hi.astype(F32)).astype(BF16)
    rbias = router_bias.reshape(N_EXPERTS, 1).astype(F32)
    ones_d = jnp.ones((1, d), F32)
    cache_krt = jnp.swapaxes(cache_krope, 2, 3)
    assert d // LANES + 1 == EXT_USED
    xs = jnp.zeros((n_moe_tiles * MOE_TILE * EXT_ROWS, LANES), F32)

    x = jnp.concatenate([x_prompt.reshape(n_prompt, d), x_sample.reshape(db, d)], 0)
    ckv_p, kr_p, pool_p, ckv_s, kr_s, pool_s = [], [], [], [], [], []
    for i in range(depth):
        j = i // 2
        if i % 2 == 0:
            wuq_p = _pad_heads(w_uq[j], QK_NOPE + QK_ROPE).astype(BF16)
            wdkv_p = jnp.concatenate(
                [w_dkv[j][:, :KV_LORA], jnp.zeros((d, ROPE_LANE0), F32), w_dkv[j][:, KV_LORA:],
                 jnp.zeros((d, LANES - ROPE_LANE0 - QK_ROPE), F32)], 1).astype(BF16)
            wuk_p = _pad_heads(w_uk[j].reshape(KV_LORA, N_HEADS * QK_NOPE), QK_NOPE).astype(BF16)
            wuv_t = w_uv[j].reshape(KV_LORA, N_HEADS * V_HEAD).T.astype(BF16)
            q_all, k_all, vt_all, ckv_all, kr_all = _mla_proj(
                x, w_dq[j].astype(BF16), row(g_q[j]), wuq_p, wdkv_p, row(g_kv[j]), wuk_p, wuv_t,
                cos_t, sin_up, sin_dn, tm)
            kr_all = kr_all[:, ROPE_LANE0:ROPE_LANE0 + QK_ROPE]
            o_all = _flash_attention(q_all, k_all, vt_all, batch, seq, ATTN_TILE)
            w_ukt = jnp.pad(jnp.transpose(w_uk[j], (1, 2, 0)), ((0, 0), (0, LANES - QK_NOPE), (0, 0))).astype(BF16)
            q_lat = jnp.transpose(_q_latent(q_all, w_ukt, db), (1, 0, 2))
            q_rope = q_all[n_prompt:].reshape(db, N_HEADS, LANES)[:, :, ROPE_LANE0:ROPE_LANE0 + QK_ROPE]
            o_lat = _paged_attention(page_table, q_lat, q_rope, ckv_all[n_prompt:].reshape(db, 1, KV_LORA),
                                     kr_all[n_prompt:].reshape(db, 1, QK_ROPE), cache_ckv, cache_krt, j, pps)
            wv = jnp.transpose(w_uv[j], (1, 0, 2))
            wv_even = jnp.pad(wv, ((0, 0), (0, 0), (0, V_HEAD)))
            wv_odd = jnp.pad(wv, ((0, 0), (0, 0), (V_HEAD, 0)))
            wv_pairs = jnp.where((jnp.arange(N_HEADS) % 2 == 0)[:, None, None], wv_even, wv_odd).astype(BF16)
            mixed = _sample_out(jnp.transpose(o_lat, (1, 0, 2)), wv_pairs, o_all, db)
            w_mix, mix_scale = w_o[j].astype(BF16), ones_d
            ckv_p.append(ckv_all[:n_prompt].reshape(batch, seq, KV_LORA))
            kr_p.append(kr_all[:n_prompt].reshape(batch, seq, QK_ROPE))
            ckv_s.append(ckv_all[n_prompt:].reshape(db, 1, KV_LORA))
            kr_s.append(kr_all[n_prompt:].reshape(db, 1, QK_ROPE))
        else:
            pooled = _pool_prompt(x, batch, seq, POOL_TILE)
            mixed = _pool_sample(jnp.transpose(state_pool[j], (1, 0, 2)).astype(F32), x, pooled, db)
            w_mix = jax.scipy.linalg.block_diag(*[w_pool[j][g] for g in range(len(POOL_WINDOWS))]).astype(BF16)
            mix_scale = row(pool_scale[j])
            pool_p.append(jnp.stack([x[(b + 1) * seq - POOL_CTX:(b + 1) * seq] for b in range(batch)]))
            pool_s.append(jnp.concatenate([state_pool[j][:, 1:].astype(F32), x[n_prompt:, None, :]], 1))
        xe, cls, rank, counts = _post_mix(
            mixed, x, w_mix, mix_scale, row(ln1_g[i]), row(ln1_b[i]), wr_hi, wr_lo, rbias, tri, alpha, tm)
        row_pos, tile_a, tile_b, n_used = _dispatch_tables(
            cls.reshape(-1), rank.reshape(-1), counts.reshape(-1).astype(jnp.int32), n_moe_tiles)
        xs = _dispatch(row_pos, xe, xs, tm)
        ys = _moe(tile_a, tile_b, n_used, xs, w_gate, w_up, w_down, i,
                  row(ln2_g[i]), row(ln2_b[i]), alpha, MOE_TILE)
        x = _unpermute(row_pos, ys, n, d, tm)
    return (x[:n_prompt].reshape(batch, seq, d), x[n_prompt:].reshape(db, 1, d),
            jnp.stack(ckv_p), jnp.stack(kr_p), jnp.stack(pool_p),
            jnp.stack(ckv_s), jnp.stack(kr_s), jnp.stack(pool_s))
```

```python
import functools
import math

import jax
import jax.numpy as jnp
from jax import lax
from jax.experimental import pallas as pl
from jax.experimental.pallas import tpu as pltpu

F32 = jnp.float32
BF16 = jnp.bfloat16

N_HEADS = 16
QK_NOPE = 64
QK_ROPE = 32
V_HEAD = 64
KV_LORA = 256
ROPE_THETA = 10000.0
ATTN_SCALE = (QK_NOPE + QK_ROPE) ** -0.5
POOL_WINDOWS = (2, 4, 8, 16)
POOL_CTX = max(POOL_WINDOWS) - 1
N_EXPERTS = 16
N_EXPERT_GROUPS = 4
EXPERTS_PER_GROUP = N_EXPERTS // N_EXPERT_GROUPS
TOP_K = 2
LN_EPS = 1e-5
RMS_EPS = 1e-6

PAIR_A = (0, 0, 0, 1, 1, 3)
PAIR_B = (1, 2, 3, 3, 2, 2)
N_PAIRS = len(PAIR_A)
N_CLASSES = N_EXPERT_GROUPS * N_PAIRS

LANES = 128
HALF_ROPE = QK_ROPE // 2
ROPE_LANE0 = QK_NOPE
HALO = 16
NEG_BIG = -1e30
LOG2E = math.log2(math.e)
VMEM_LIMIT = 52 * 1024 * 1024
MOE_TILE = 256
DISPATCH_TILE_MAX = 1408
ATTN_TILE = 512
ATTN_HEADS_PER_STEP = 4
ONES_ROWS = 16
POOL_TILE = 512
PAGES_PER_STEP = 16
SUBLANES = 8
EXT_ROWS = 2 * SUBLANES
EXT_USED = SUBLANES + 1
NT_DIMS = (((1,), (1,)), ((), ()))


def _params(*sem):
    return pltpu.CompilerParams(dimension_semantics=sem, vmem_limit_bytes=VMEM_LIMIT)


def _rms(x, g):
    return x * lax.rsqrt(jnp.mean(x * x, -1, keepdims=True) + RMS_EPS) * g


def _layer_norm(y, g, b):
    mu = jnp.mean(y, -1, keepdims=True)
    d = y - mu
    var = jnp.mean(d * d, -1, keepdims=True)
    return d * lax.rsqrt(var + LN_EPS) * g + b


def _load_chunks(ref, tokens, rows, n_chunks):
    return jnp.concatenate([ref[pl.ds(c, tokens, stride=rows), :] for c in range(n_chunks)], axis=1)


def _store_chunks(ref, val, rows):
    tokens, width = val.shape
    for c in range(width // LANES):
        ref[pl.ds(c, tokens, stride=rows), :] = val[:, c * LANES:(c + 1) * LANES]


def _rope_chunk(c, cos, sin_up, sin_dn):
    return c * cos + pltpu.roll(c, HALF_ROPE, 1) * sin_up + pltpu.roll(c, LANES - HALF_ROPE, 1) * sin_dn


def _mla_proj_kernel(x_ref, wdq_ref, gq_ref, wuq_ref, wdkv_ref, gkv_ref, wuk_ref, wuvt_ref,
                     cos_ref, sup_ref, sdn_ref, q_ref, k_ref, vt_ref, ckv_ref, kr_ref):
    xb = x_ref[...].astype(BF16)
    cos, sup, sdn = cos_ref[...], sup_ref[...], sdn_ref[...]
    cq = _rms(jnp.dot(xb, wdq_ref[...], preferred_element_type=F32), gq_ref[...])
    q = jnp.dot(cq.astype(BF16), wuq_ref[...], preferred_element_type=F32)
    kv = jnp.dot(xb, wdkv_ref[...], preferred_element_type=F32)
    ckv = _rms(kv[:, :KV_LORA], gkv_ref[...])
    ckv_ref[...] = ckv
    kr = _rope_chunk(kv[:, KV_LORA:], cos, sup, sdn)
    kr_ref[...] = kr
    ckv_b = ckv.astype(BF16)
    kn = jnp.dot(ckv_b, wuk_ref[...], preferred_element_type=F32)
    vt_ref[...] = lax.dot_general(wuvt_ref[...], ckv_b, NT_DIMS, preferred_element_type=F32).astype(BF16)
    for h in range(N_HEADS):
        sl = slice(h * LANES, (h + 1) * LANES)
        q_ref[:, sl] = (_rope_chunk(q[:, sl], cos, sup, sdn) * (ATTN_SCALE * LOG2E)).astype(BF16)
        k_ref[:, sl] = (kn[:, sl] + kr).astype(BF16)


def _mla_proj(x, wdq, gq, wuq, wdkv, gkv, wuk, wuvt, cos, sup, sdn, tm):
    n, d = x.shape
    full = lambda a: pl.BlockSpec(a.shape, lambda i: (0,) * a.ndim)
    rows = lambda w: pl.BlockSpec((tm, w), lambda i: (i, 0))
    hp = N_HEADS * LANES
    hv = N_HEADS * V_HEAD
    return pl.pallas_call(
        _mla_proj_kernel,
        grid=(n // tm,),
        in_specs=[rows(d), full(wdq), full(gq), full(wuq), full(wdkv), full(gkv), full(wuk), full(wuvt),
                  rows(LANES), rows(LANES), rows(LANES)],
        out_specs=[rows(hp), rows(hp), pl.BlockSpec((hv, tm), lambda i: (0, i)), rows(KV_LORA), rows(LANES)],
        out_shape=[jax.ShapeDtypeStruct((n, hp), BF16), jax.ShapeDtypeStruct((n, hp), BF16),
                   jax.ShapeDtypeStruct((hv, n), BF16),
                   jax.ShapeDtypeStruct((n, KV_LORA), F32), jax.ShapeDtypeStruct((n, LANES), F32)],
        compiler_params=_params("parallel"),
        name="mla_proj",
    )(x, wdq, gq, wuq, wdkv, gkv, wuk, wuvt, cos, sup, sdn)


def _flash_kernel(q_ref, k_ref, vt_ref, o_ref, *, tile, heads):
    qi = pl.program_id(2)
    krow = lax.broadcasted_iota(jnp.int32, (tile, tile), 0)
    qcol = lax.broadcasted_iota(jnp.int32, (tile, tile), 1)
    ones = jnp.ones((ONES_ROWS, tile), BF16)

    def step(kv, carry, masked):
        start = pl.multiple_of(kv * tile, tile)
        new = []
        scores = [lax.dot_general(k_ref[pl.ds(start, tile), hh * LANES:(hh + 1) * LANES],
                                  q_ref[:, hh * LANES:(hh + 1) * LANES], NT_DIMS, preferred_element_type=F32)
                  for hh in range(heads)]
        for hh in range(heads):
            m, acc = carry[hh]
            s = scores[hh]
            if masked:
                s = jnp.where(krow <= qcol, s, NEG_BIG)
            m_new = jnp.maximum(m, jnp.max(s, 0, keepdims=True))
            p = jnp.exp2(s - m_new).astype(BF16)
            a = jnp.exp2(m - m_new)
            vt = jnp.concatenate([vt_ref[hh * V_HEAD:(hh + 1) * V_HEAD, pl.ds(start, tile)], ones], axis=0)
            acc = a * acc + jnp.dot(vt, p, preferred_element_type=F32)
            new.append((m_new, acc))
        return tuple(new)

    init = tuple((jnp.full((1, tile), NEG_BIG, F32), jnp.zeros((V_HEAD + ONES_ROWS, tile), F32))
                 for _ in range(heads))
    carry = lax.fori_loop(0, qi, lambda kv, c: step(kv, c, False), init)
    carry = step(qi, carry, True)
    o_t = jnp.concatenate([acc[:V_HEAD] / acc[V_HEAD:V_HEAD + 1] for _, acc in carry], axis=0)
    o_ref[...] = o_t.T.astype(BF16)


def _flash_attention(q, k, vt, batch, seq, tile):
    n = q.shape[0]
    nq = seq // tile
    hps = ATTN_HEADS_PER_STEP
    return pl.pallas_call(
        functools.partial(_flash_kernel, tile=tile, heads=hps),
        grid=(batch, N_HEADS // hps, nq),
        in_specs=[pl.BlockSpec((tile, hps * LANES), lambda b, p, i: (b * nq + i, p)),
                  pl.BlockSpec((seq, hps * LANES), lambda b, p, i: (b, p)),
                  pl.BlockSpec((hps * V_HEAD, seq), lambda b, p, i: (p, b))],
        out_specs=pl.BlockSpec((tile, hps * V_HEAD), lambda b, p, i: (b * nq + i, p)),
        out_shape=jax.ShapeDtypeStruct((n, N_HEADS * V_HEAD), BF16),
        compiler_params=_params("parallel", "parallel", "arbitrary"),
        name="flash_prompt",
    )(q, k, vt)


def _qlat_kernel(q_ref, w_ref, o_ref):
    o_ref[...] = jnp.dot(q_ref[...], w_ref[...], preferred_element_type=F32).astype(BF16)


def _q_latent(q, w_ukt, db):
    n = q.shape[0]
    blk = n // db - 1
    return pl.pallas_call(
        _qlat_kernel,
        grid=(N_HEADS,),
        in_specs=[pl.BlockSpec((db, LANES), lambda h: (blk, h)),
                  pl.BlockSpec((None, LANES, KV_LORA), lambda h: (h, 0, 0))],
        out_specs=pl.BlockSpec((None, db, KV_LORA), lambda h: (h, 0, 0)),
        out_shape=jax.ShapeDtypeStruct((N_HEADS, db, KV_LORA), BF16),
        compiler_params=_params("parallel"),
        name="q_latent",
    )(q, w_ukt)


def _paged_kernel(pt_ref, ql_ref, qr_ref, cnew_ref, knew_ref, ckv_hbm, krt_hbm, o_ref, kbuf, rbuf, sem,
                  *, layer, n_pages, psz, pps):
    b = pl.program_id(0)

    def page_copies(seq, slot):
        for i in range(n_pages):
            page = pt_ref[seq * n_pages + i]
            yield pltpu.make_async_copy(ckv_hbm.at[layer, page], kbuf.at[slot, pl.ds(i * psz, psz), :],
                                        sem.at[slot])
            yield pltpu.make_async_copy(krt_hbm.at[layer, page], rbuf.at[slot, :, pl.ds(i * psz, psz)],
                                        sem.at[slot])

    @pl.when(b == 0)
    def _():
        for cp in page_copies(0, 0):
            cp.start()

    @pl.when(b + 1 < pl.num_programs(0))
    def _():
        for cp in page_copies(b + 1, (b + 1) % 2):
            cp.start()

    slot = b % 2
    for cp in page_copies(b, slot):
        cp.wait()

    ql = ql_ref[...]
    qr = qr_ref[...]
    width = pps * psz
    chunks = range(n_pages // pps)
    keys = [kbuf[slot, c * width:(c + 1) * width, :].astype(BF16) for c in chunks]
    krts = [rbuf[slot, :, c * width:(c + 1) * width].astype(BF16) for c in chunks]
    scores = [lax.dot_general(ql, keys[c], NT_DIMS, preferred_element_type=F32)
              + jnp.dot(qr, krts[c], preferred_element_type=F32) for c in chunks]
    maxes = [jnp.max(s, -1, keepdims=True) for s in scores]
    probs = [jnp.exp2(scores[c] - maxes[c]).astype(BF16) for c in chunks]
    parts = [(maxes[c], jnp.sum(probs[c].astype(F32), -1, keepdims=True),
              jnp.dot(probs[c], keys[c], preferred_element_type=F32)) for c in chunks]
    cn = cnew_ref[...].astype(BF16).astype(F32)
    kn = knew_ref[...].astype(BF16).astype(F32)
    s_self = (jnp.sum(ql.astype(F32) * cn, -1, keepdims=True) + jnp.sum(qr.astype(F32) * kn, -1, keepdims=True))
    m = functools.reduce(jnp.maximum, [pt[0] for pt in parts] + [s_self])
    p_self = jnp.exp2(s_self - m).astype(BF16).astype(F32)
    l = p_self
    acc = p_self * cn
    for m_c, l_c, acc_c in parts:
        a = jnp.exp2(m_c - m)
        l = l + a * l_c
        acc = acc + a * acc_c
    o_ref[...] = acc / l


def _paged_attention(page_table, q_lat, q_rope, c_new, k_new, cache_ckv, cache_krt, layer, pps):
    db, n_pages = page_table.shape
    psz = cache_ckv.shape[2]
    pt = page_table.reshape(-1)
    per_b = lambda r, w: pl.BlockSpec((None, r, w), lambda b, pt_ref: (b, 0, 0))
    grid_spec = pltpu.PrefetchScalarGridSpec(
        num_scalar_prefetch=1,
        grid=(db,),
        in_specs=[per_b(N_HEADS, KV_LORA), per_b(N_HEADS, QK_ROPE), per_b(1, KV_LORA), per_b(1, QK_ROPE),
                  pl.BlockSpec(memory_space=pl.ANY), pl.BlockSpec(memory_space=pl.ANY)],
        out_specs=per_b(N_HEADS, KV_LORA),
        scratch_shapes=[pltpu.VMEM((2, n_pages * psz, KV_LORA), cache_ckv.dtype),
                        pltpu.VMEM((2, QK_ROPE, n_pages * psz), cache_krt.dtype),
                        pltpu.SemaphoreType.DMA((2,))],
    )
    return pl.pallas_call(
        functools.partial(_paged_kernel, layer=layer, n_pages=n_pages, psz=psz, pps=pps),
        grid_spec=grid_spec,
        out_shape=jax.ShapeDtypeStruct((db, N_HEADS, KV_LORA), F32),
        compiler_params=_params("arbitrary"),
        name="paged_sample",
    )(pt, q_lat, q_rope, c_new, k_new, cache_ckv, cache_krt)


def _sample_out_kernel(ol_ref, w_ref, oin_ref, o_ref):
    del oin_ref
    acc = jnp.dot(ol_ref[0].astype(BF16), w_ref[0], preferred_element_type=F32)
    acc = acc + jnp.dot(ol_ref[1].astype(BF16), w_ref[1], preferred_element_type=F32)
    o_ref[...] = acc.astype(BF16)


def _sample_out(o_lat_t, w_uv_pairs, o_all, db):
    n = o_all.shape[0]
    blk = n // db - 1
    return pl.pallas_call(
        _sample_out_kernel,
        grid=(N_HEADS // 2,),
        in_specs=[pl.BlockSpec((2, db, KV_LORA), lambda p: (p, 0, 0)),
                  pl.BlockSpec((2, KV_LORA, 2 * V_HEAD), lambda p: (p, 0, 0)),
                  pl.BlockSpec(memory_space=pl.ANY)],
        out_specs=pl.BlockSpec((db, 2 * V_HEAD), lambda p: (blk, p)),
        out_shape=jax.ShapeDtypeStruct(o_all.shape, o_all.dtype),
        input_output_aliases={2: 0},
        compiler_params=_params("parallel"),
        name="sample_out",
    )(o_lat_t, w_uv_pairs, o_all)


def _pool_prompt_kernel(x_ref, halo_ref, o_ref, l0, l1, l2, l3, *, tile, seq):
    i = pl.program_id(0)
    d = x_ref.shape[1]
    gw = d // len(POOL_WINDOWS)
    ext = tile + HALO
    s0 = (i * tile) % seq
    pos_ext = s0 - HALO + lax.broadcasted_iota(jnp.int32, (ext, 1), 0)
    zeros = jnp.zeros((HALO, d), F32)
    for buf in (l0, l1, l2, l3):
        buf[0:HALO, :] = zeros
    l0[HALO:2 * HALO, :] = halo_ref[...]
    l0[2 * HALO:, :] = x_ref[...]

    def level(src, dst, k, c0):
        cur = src[HALO:HALO + ext, c0:]
        sh = src[HALO - k:HALO - k + ext, c0:]
        dst[HALO:HALO + ext, c0:] = cur + jnp.where(pos_ext >= k, sh, 0.0)

    level(l0, l1, 1, 0)
    level(l1, l2, 2, gw)
    level(l2, l3, 4, 2 * gw)
    pos = pos_ext[HALO:]
    x = x_ref[...]
    t0 = 2 * HALO
    sums = (l1[t0:, 0:gw], l2[t0:, gw:2 * gw], l3[t0:, 2 * gw:3 * gw],
            l3[t0:, 3 * gw:] + jnp.where(pos >= 8, l3[t0 - 8:t0 - 8 + tile, 3 * gw:], 0.0))
    for g, w in enumerate(POOL_WINDOWS):
        cnt = jnp.minimum(pos + 1, w).astype(F32)
        o_ref[:, g * gw:(g + 1) * gw] = (sums[g] / cnt - x[:, g * gw:(g + 1) * gw]).astype(BF16)


def _pool_prompt(x, batch, seq, tile):
    n, d = x.shape
    hb = tile // HALO
    return pl.pallas_call(
        functools.partial(_pool_prompt_kernel, tile=tile, seq=seq),
        grid=(batch * seq // tile,),
        in_specs=[pl.BlockSpec((tile, d), lambda i: (i, 0)),
                  pl.BlockSpec((HALO, d), lambda i: (jnp.maximum(i * hb - 1, 0), 0))],
        out_specs=pl.BlockSpec((tile, d), lambda i: (i, 0)),
        out_shape=jax.ShapeDtypeStruct((n, d), BF16),
        scratch_shapes=[pltpu.VMEM((tile + 2 * HALO, d), F32)] * 4,
        compiler_params=_params("parallel"),
        name="pool_prompt",
    )(x, x)


def _pool_sample_kernel(st_ref, x_ref, pin_ref, o_ref):
    del pin_ref
    d = x_ref.shape[1]
    gw = d // len(POOL_WINDOWS)
    x = x_ref[...]
    for g, w in enumerate(POOL_WINDOWS):
        cs = slice(g * gw, (g + 1) * gw)
        acc = x[:, cs]
        for r in range(POOL_CTX - (w - 1), POOL_CTX):
            acc = acc + st_ref[r, :, cs]
        o_ref[:, cs] = (acc / float(w) - x[:, cs]).astype(BF16)


def _pool_sample(state_t, x, pooled, db):
    n, d = x.shape
    blk = n // db - 1
    return pl.pallas_call(
        _pool_sample_kernel,
        grid=(1,),
        in_specs=[pl.BlockSpec(state_t.shape, lambda i: (0, 0, 0)),
                  pl.BlockSpec((db, d), lambda i: (blk, 0)),
                  pl.BlockSpec(memory_space=pl.ANY)],
        out_specs=pl.BlockSpec((db, d), lambda i: (blk, 0)),
        out_shape=jax.ShapeDtypeStruct(pooled.shape, pooled.dtype),
        input_output_aliases={2: 0},
        compiler_params=_params("arbitrary"),
        name="pool_sample",
    )(state_t, x, pooled)


def _route_rows(logits_t, bias_ref):
    s = 1.0 / (1.0 + jnp.exp(-logits_t))
    sr = [s[e:e + 1, :] for e in range(N_EXPERTS)]
    br = [sr[e] + bias_ref[e:e + 1, :] for e in range(N_EXPERTS)]
    gscore = []
    for g in range(N_EXPERT_GROUPS):
        r = br[g * EXPERTS_PER_GROUP:(g + 1) * EXPERTS_PER_GROUP]
        best2 = None
        for a in range(EXPERTS_PER_GROUP):
            for b in range(a + 1, EXPERTS_PER_GROUP):
                pair = r[a] + r[b]
                best2 = pair if best2 is None else jnp.maximum(best2, pair)
        gscore.append(best2)
    top = functools.reduce(jnp.maximum, gscore)
    best = jnp.full(top.shape, N_EXPERT_GROUPS - 1, jnp.int32)
    for g in range(N_EXPERT_GROUPS - 2, -1, -1):
        best = jnp.where(gscore[g] == top, g, best)
    sel = []
    for e in range(N_EXPERTS):
        g = e // EXPERTS_PER_GROUP
        ahead = jnp.zeros(top.shape, F32)
        for o in range(g * EXPERTS_PER_GROUP, (g + 1) * EXPERTS_PER_GROUP):
            if o == e:
                continue
            beats = (br[o] >= br[e]) if o < e else (br[o] > br[e])
            ahead = ahead + jnp.where(beats, 1.0, 0.0)
        sel.append(jnp.where(ahead < float(TOP_K), 1.0, 0.0) * jnp.where(best == g, 1.0, 0.0))
    den = functools.reduce(lambda a, b: a + b, [sel[e] * sr[e] for e in range(N_EXPERTS)])
    wts = [sel[e] * sr[e] / den for e in range(N_EXPERTS)]
    return sel, wts


def _post_mix_kernel(a_ref, x_ref, w_ref, sc_ref, g_ref, b_ref, wrh_ref, wrl_ref, rb_ref, tri_ref,
                     xe_ref, cls_ref, rank_ref, cnt_ref, cnt_scr, *, alpha):
    i = pl.program_id(0)
    d = x_ref.shape[1]

    @pl.when(i == 0)
    def _():
        cnt_scr[...] = jnp.zeros(cnt_scr.shape, F32)

    if len(w_ref.shape) == 3:
        gw = w_ref.shape[1]
        mix = jnp.concatenate([jnp.dot(a_ref[:, g * gw:(g + 1) * gw], w_ref[g], preferred_element_type=F32)
                               for g in range(w_ref.shape[0])], axis=1)
    else:
        mix = jnp.dot(a_ref[...], w_ref[...], preferred_element_type=F32)
    x1 = _layer_norm(alpha * x_ref[...] + mix * sc_ref[...], g_ref[...], b_ref[...])
    _store_chunks(xe_ref, x1, EXT_ROWS)
    x_hi = x1.astype(BF16)
    x_lo = (x1 - x_hi.astype(F32)).astype(BF16)
    wrh = wrh_ref[...]
    logits_t = (lax.dot_general(wrh, x_hi, NT_DIMS, preferred_element_type=F32)
                + lax.dot_general(wrh, x_lo, NT_DIMS, preferred_element_type=F32)
                + lax.dot_general(wrl_ref[...], x_hi, NT_DIMS, preferred_element_type=F32))
    sel, wts = _route_rows(logits_t, rb_ref)
    t = logits_t.shape[1]
    ind = []
    w_a = jnp.zeros(sel[0].shape, F32)
    w_b = jnp.zeros(sel[0].shape, F32)
    cls = jnp.zeros(sel[0].shape, F32)
    for g in range(N_EXPERT_GROUPS):
        for k in range(N_PAIRS):
            ea = g * EXPERTS_PER_GROUP + PAIR_A[k]
            eb = g * EXPERTS_PER_GROUP + PAIR_B[k]
            hit = sel[ea] * sel[eb]
            w_a = w_a + hit * wts[ea]
            w_b = w_b + hit * wts[eb]
            cls = cls + hit * float(len(ind))
            ind.append(hit)
    crow = lax.broadcasted_iota(jnp.int32, (N_CLASSES, t), 0)
    ind_t = jnp.zeros((N_CLASSES, t), F32)
    for c in range(N_CLASSES):
        ind_t = jnp.where(crow == c, ind[c], ind_t)
    before = jnp.dot(ind_t.astype(BF16), tri_ref[...], preferred_element_type=F32)
    rank_t = cnt_scr[...] + before
    cnt_new = cnt_scr[...] + jnp.sum(ind_t, -1, keepdims=True)
    cnt_scr[...] = cnt_new
    cnt_ref[...] = cnt_new
    rank = jnp.zeros(sel[0].shape, F32)
    for c in range(N_CLASSES):
        rank = rank + ind[c] * rank_t[c:c + 1, :]
    cls_ref[...] = cls.astype(jnp.int32)
    rank_ref[...] = rank.astype(jnp.int32)
    lrow = lax.broadcasted_iota(jnp.int32, (LANES, t), 0)
    w_t = jnp.where(lrow == 0, w_a, jnp.where(lrow == 1, w_b, 0.0))
    xe_ref[pl.ds(EXT_USED - 1, t, stride=EXT_ROWS), :] = w_t.T


def _post_mix(a, x, w, scale, g, b, wr_hi, wr_lo, rbias, tri, alpha, tm):
    n, d = x.shape
    full = lambda t: pl.BlockSpec(t.shape, lambda i: (0,) * t.ndim)
    rows = lambda wd: pl.BlockSpec((tm, wd), lambda i: (i, 0))
    cols = pl.BlockSpec((1, tm), lambda i: (0, i))
    return pl.pallas_call(
        functools.partial(_post_mix_kernel, alpha=alpha),
        grid=(n // tm,),
        in_specs=[rows(d), rows(d), full(w), full(scale), full(g), full(b), full(wr_hi), full(wr_lo),
                  full(rbias), full(tri)],
        out_specs=[pl.BlockSpec((tm * EXT_ROWS, LANES), lambda i: (i, 0)), cols, cols,
                   pl.BlockSpec((N_CLASSES, 1), lambda i: (0, 0))],
        out_shape=[jax.ShapeDtypeStruct((n * EXT_ROWS, LANES), F32),
                   jax.ShapeDtypeStruct((1, n), jnp.int32), jax.ShapeDtypeStruct((1, n), jnp.int32),
                   jax.ShapeDtypeStruct((N_CLASSES, 1), F32)],
        scratch_shapes=[pltpu.VMEM((N_CLASSES, 1), F32)],
        compiler_params=_params("arbitrary"),
        name="post_mix",
    )(a, x, w, scale, g, b, wr_hi, wr_lo, rbias, tri)


def _token_copy(src_ref, src_tok, dst_ref, dst_tok, rows, sem, used=None):
    used = rows if used is None else used
    src = src_ref.at[pl.ds(pl.multiple_of(src_tok * rows, rows), used), :]
    dst = dst_ref.at[pl.ds(pl.multiple_of(dst_tok * rows, rows), used), :]
    return pltpu.make_async_copy(src, dst, sem)


def _wait_tokens(src_ref, dst_ref, tokens, rows, sem):
    span = pl.ds(0, tokens * rows)
    pltpu.make_async_copy(src_ref.at[span, :], dst_ref.at[span, :], sem).wait()


def _dispatch_kernel(pos_ref, x_ref, xs_in_ref, xs_ref, sem, *, tm):
    del xs_in_ref
    base = pl.program_id(0) * tm

    def pair(r2, carry):
        for u in range(2):
            r = 2 * r2 + u
            _token_copy(x_ref, r, xs_ref, pos_ref[base + r], EXT_ROWS, sem, used=EXT_USED).start(priority=u)
        return carry

    lax.fori_loop(0, tm // 2, pair, 0, unroll=4)
    _wait_tokens(x_ref, xs_ref, tm, EXT_USED, sem)


def _dispatch(pos, xe, xs_init, tm):
    n = xe.shape[0] // EXT_ROWS
    assert tm % 2 == 0
    grid_spec = pltpu.PrefetchScalarGridSpec(
        num_scalar_prefetch=1,
        grid=(n // tm,),
        in_specs=[pl.BlockSpec((tm * EXT_ROWS, LANES), lambda i, p: (i, 0)), pl.BlockSpec(memory_space=pl.ANY)],
        out_specs=pl.BlockSpec(memory_space=pl.ANY),
        scratch_shapes=[pltpu.SemaphoreType.DMA(())],
    )
    return pl.pallas_call(
        functools.partial(_dispatch_kernel, tm=tm),
        grid_spec=grid_spec,
        out_shape=jax.ShapeDtypeStruct(xs_init.shape, xs_init.dtype),
        input_output_aliases={2: 0},
        compiler_params=_params("arbitrary"),
        name="moe_dispatch",
    )(pos, xe, xs_init)


def _unpermute_kernel(pos_ref, xs_ref, o_ref, buf, sem, *, tm):
    i = pl.program_id(0)
    rows = o_ref.shape[1] // LANES

    def issue(tile, slot):
        base = tile * tm

        def pair(r2, carry):
            for u in range(2):
                r = 2 * r2 + u
                _token_copy(xs_ref, pos_ref[base + r], buf.at[slot], r, rows, sem.at[slot]).start(priority=u)
            return carry

        lax.fori_loop(0, tm // 2, pair, 0, unroll=4)

    @pl.when(i == 0)
    def _():
        issue(0, 0)

    @pl.when(i + 1 < pl.num_programs(0))
    def _():
        issue(i + 1, (i + 1) % 2)

    slot = i % 2
    _wait_tokens(xs_ref, buf.at[slot], tm, rows, sem.at[slot])
    o_ref[...] = _load_chunks(buf.at[slot], tm, rows, rows)


def _unpermute(pos, xs, n, d, tm):
    rows = d // LANES
    assert tm % 2 == 0
    grid_spec = pltpu.PrefetchScalarGridSpec(
        num_scalar_prefetch=1,
        grid=(n // tm,),
        in_specs=[pl.BlockSpec(memory_space=pl.ANY)],
        out_specs=pl.BlockSpec((tm, d), lambda i, p: (i, 0)),
        scratch_shapes=[pltpu.VMEM((2, tm * rows, LANES), xs.dtype), pltpu.SemaphoreType.DMA((2,))],
    )
    return pl.pallas_call(
        functools.partial(_unpermute_kernel, tm=tm),
        grid_spec=grid_spec,
        out_shape=jax.ShapeDtypeStruct((n, d), xs.dtype),
        compiler_params=_params("arbitrary"),
        name="moe_unpermute",
    )(pos, xs)


def _moe_kernel(ta_ref, tb_ref, nu_ref, x_ref, wga_ref, wua_ref, wda_ref, wgb_ref, wub_ref, wdb_ref,
                g_ref, b_ref, o_ref, wgu_s, wd_s, *, alpha, tile):
    i = pl.program_id(0)
    n_chunks = o_ref.shape[0] // tile
    f = wga_ref.shape[1]

    @pl.when(i < nu_ref[0])
    def _():
        prev = jnp.maximum(i - 1, 0)

        @pl.when((i == 0) | (ta_ref[i] != ta_ref[prev]))
        def _():
            wgu_s[:, 0:f] = wga_ref[...].astype(BF16)
            wgu_s[:, f:2 * f] = wua_ref[...].astype(BF16)
            wd_s[0:f, :] = wda_ref[...].astype(BF16)

        @pl.when((i == 0) | (tb_ref[i] != tb_ref[prev]))
        def _():
            wgu_s[:, 2 * f:3 * f] = wgb_ref[...].astype(BF16)
            wgu_s[:, 3 * f:4 * f] = wub_ref[...].astype(BF16)
            wd_s[f:2 * f, :] = wdb_ref[...].astype(BF16)

        x = _load_chunks(x_ref, tile, EXT_ROWS, n_chunks)
        combine = x_ref[pl.ds(n_chunks, tile, stride=EXT_ROWS), :]
        gu = jnp.dot(x.astype(BF16), wgu_s[...], preferred_element_type=F32)
        hs = []
        for col in range(TOP_K):
            gate = gu[:, 2 * col * f:(2 * col + 1) * f]
            up = gu[:, (2 * col + 1) * f:(2 * col + 2) * f]
            hs.append((gate * (1.0 / (1.0 + jnp.exp(-gate))) * up * combine[:, col:col + 1]).astype(BF16))
        y = jnp.dot(jnp.concatenate(hs, axis=1), wd_s[...], preferred_element_type=F32)
        _store_chunks(o_ref, _layer_norm(alpha * x + y, g_ref[...], b_ref[...]), n_chunks)


def _moe(tile_a, tile_b, n_used, xs, w_gate, w_up, w_down, layer, g, b, alpha, tile):
    p = xs.shape[0] // EXT_ROWS
    d, f = w_gate.shape[-2:]
    ch = d // LANES
    row_blk = lambda i, ta, tb, nu: (jnp.minimum(i, nu[0] - 1), 0)
    wa = lambda r, c: pl.BlockSpec((None, None, r, c), lambda i, ta, tb, nu: (layer, ta[i], 0, 0))
    wb = lambda r, c: pl.BlockSpec((None, None, r, c), lambda i, ta, tb, nu: (layer, tb[i], 0, 0))
    full = lambda t: pl.BlockSpec(t.shape, lambda i, ta, tb, nu: (0,) * t.ndim)
    grid_spec = pltpu.PrefetchScalarGridSpec(
        num_scalar_prefetch=3,
        grid=(p // tile,),
        in_specs=[pl.BlockSpec((tile * EXT_ROWS, LANES), row_blk), wa(d, f), wa(d, f), wa(f, d),
                  wb(d, f), wb(d, f), wb(f, d), full(g), full(b)],
        out_specs=pl.BlockSpec((tile * ch, LANES), row_blk),
        scratch_shapes=[pltpu.VMEM((d, 2 * TOP_K * f), BF16), pltpu.VMEM((TOP_K * f, d), BF16)],
    )
    return pl.pallas_call(
        functools.partial(_moe_kernel, alpha=alpha, tile=tile),
        grid_spec=grid_spec,
        out_shape=jax.ShapeDtypeStruct((p * ch, LANES), F32),
        compiler_params=_params("arbitrary"),
        name="moe_grouped",
    )(tile_a, tile_b, n_used, xs, w_gate, w_up, w_down, w_gate, w_up, w_down, g, b)


def _dispatch_tables(cls, rank, counts, n_tiles):
    padded = ((counts + MOE_TILE - 1) // MOE_TILE) * MOE_TILE
    ends = jnp.cumsum(padded)
    pos = (ends - padded)[cls] + rank
    tile_start = jnp.arange(n_tiles, dtype=jnp.int32) * MOE_TILE
    tile_cls = jnp.minimum(jnp.sum((tile_start[:, None] >= ends[None, :]).astype(jnp.int32), 1), N_CLASSES - 1)
    group0 = (tile_cls // N_PAIRS) * EXPERTS_PER_GROUP
    tile_a = group0 + jnp.asarray(PAIR_A, jnp.int32)[tile_cls % N_PAIRS]
    tile_b = group0 + jnp.asarray(PAIR_B, jnp.int32)[tile_cls % N_PAIRS]
    n_used = (ends[-1:] // MOE_TILE).astype(jnp.int32)
    return pos.astype(jnp.int32), tile_a.astype(jnp.int32), tile_b.astype(jnp.int32), n_used


def _rope_tables(pos):
    inv = ROPE_THETA ** (-2.0 * jnp.arange(HALF_ROPE, dtype=F32) / QK_ROPE)
    ang = pos.astype(F32)[:, None] * inv[None, :]
    cos, sin = jnp.cos(ang), jnp.sin(ang)
    n = pos.shape[0]
    ones = jnp.ones((n, ROPE_LANE0), F32)
    zeros = jnp.zeros((n, ROPE_LANE0), F32)
    tail1 = jnp.ones((n, LANES - ROPE_LANE0 - QK_ROPE), F32)
    tail0 = jnp.zeros((n, LANES - ROPE_LANE0 - QK_ROPE), F32)
    zh = jnp.zeros((n, HALF_ROPE), F32)
    cos_t = jnp.concatenate([ones, cos, cos, tail1], 1)
    sin_up = jnp.concatenate([zeros, zh, sin, tail0], 1)
    sin_dn = jnp.concatenate([zeros, -sin, zh, tail0], 1)
    return cos_t, sin_up, sin_dn


def _pad_heads(w, per_head):
    k = w.shape[0]
    w = w.reshape(k, N_HEADS, per_head)
    return jnp.pad(w, ((0, 0), (0, 0), (0, LANES - per_head))).reshape(k, N_HEADS * LANES)


def kernel(x_prompt, x_sample, cache_ckv, cache_krope, state_pool, page_table, w_dq, g_q, w_uq, w_dkv, g_kv,
           w_uk, w_uv, w_o, w_pool, pool_scale, w_router, router_bias, w_gate, w_up, w_down,
           ln1_g, ln1_b, ln2_g, ln2_b):
    batch, seq, d = x_prompt.shape
    db, dec_t, _ = x_sample.shape
    assert dec_t == 1 and d % LANES == 0 and seq % ATTN_TILE == 0 and seq % POOL_TILE == 0
    depth = ln1_g.shape[0]
    alpha = (2 * depth) ** 0.25
    n_prompt = batch * seq
    n = n_prompt + db
    assert n % db == 0 and db % 8 == 0
    tm = 384 if n % 384 == 0 else 128
    assert n % tm == 0
    tm_dispatch = max(t for t in range(2, DISPATCH_TILE_MAX + 1, 2) if n % t == 0)
    n_pages = page_table.shape[1]
    past_len = n_pages * cache_ckv.shape[2]
    pps = min(PAGES_PER_STEP, n_pages)
    assert n_pages % pps == 0
    n_moe_tiles = (n + N_CLASSES * (MOE_TILE - 1)) // MOE_TILE
    row = lambda v: v.reshape(1, -1).astype(F32)

    tables = _rope_tables(jnp.concatenate([jnp.arange(seq), jnp.full((1,), past_len)]))
    cos_t, sin_up, sin_dn = [jnp.concatenate([jnp.tile(t[:seq], (batch, 1)), jnp.tile(t[seq:], (db, 1))])
                             for t in tables]
    tri = (jnp.arange(tm)[:, None] < jnp.arange(tm)[None, :]).astype(BF16)
    wr_t = w_router.T.astype(F32)
    wr_hi = wr_t.astype(BF16)
    wr_lo = (wr_t - wr_hi.astype(F32)).astype(BF16)
    rbias = router_bias.reshape(N_EXPERTS, 1).astype(F32)
    ones_d = jnp.ones((1, d), F32)
    cache_krt = jnp.swapaxes(cache_krope, 2, 3)
    assert d // LANES + 1 == EXT_USED
    xs = jnp.zeros((n_moe_tiles * MOE_TILE * EXT_ROWS, LANES), F32)

    x = jnp.concatenate([x_prompt.reshape(n_prompt, d), x_sample.reshape(db, d)], 0)
    ckv_p, kr_p, pool_p, ckv_s, kr_s, pool_s = [], [], [], [], [], []
    for i in range(depth):
        j = i // 2
        if i % 2 == 0:
            wuq_p = _pad_heads(w_uq[j], QK_NOPE + QK_ROPE).astype(BF16)
            wdkv_p = jnp.concatenate(
                [w_dkv[j][:, :KV_LORA], jnp.zeros((d, ROPE_LANE0), F32), w_dkv[j][:, KV_LORA:],
                 jnp.zeros((d, LANES - ROPE_LANE0 - QK_ROPE), F32)], 1).astype(BF16)
            wuk_p = _pad_heads(w_uk[j].reshape(KV_LORA, N_HEADS * QK_NOPE), QK_NOPE).astype(BF16)
            wuv_t = w_uv[j].reshape(KV_LORA, N_HEADS * V_HEAD).T.astype(BF16)
            q_all, k_all, vt_all, ckv_all, kr_all = _mla_proj(
                x, w_dq[j].astype(BF16), row(g_q[j]), wuq_p, wdkv_p, row(g_kv[j]), wuk_p, wuv_t,
                cos_t, sin_up, sin_dn, tm)
            kr_all = kr_all[:, ROPE_LANE0:ROPE_LANE0 + QK_ROPE]
            o_all = _flash_attention(q_all, k_all, vt_all, batch, seq, ATTN_TILE)
            w_ukt = jnp.pad(jnp.transpose(w_uk[j], (1, 2, 0)), ((0, 0), (0, LANES - QK_NOPE), (0, 0))).astype(BF16)
            q_lat = jnp.transpose(_q_latent(q_all, w_ukt, db), (1, 0, 2))
            q_rope = q_all[n_prompt:].reshape(db, N_HEADS, LANES)[:, :, ROPE_LANE0:ROPE_LANE0 + QK_ROPE]
            o_lat = _paged_attention(page_table, q_lat, q_rope, ckv_all[n_prompt:].reshape(db, 1, KV_LORA),
                                     kr_all[n_prompt:].reshape(db, 1, QK_ROPE), cache_ckv, cache_krt, j, pps)
            wv = jnp.transpose(w_uv[j], (1, 0, 2))
            wv_even = jnp.pad(wv, ((0, 0), (0, 0), (0, V_HEAD)))
            wv_odd = jnp.pad(wv, ((0, 0), (0, 0), (V_HEAD, 0)))
            wv_pairs = jnp.where((jnp.arange(N_HEADS) % 2 == 0)[:, None, None], wv_even, wv_odd).astype(BF16)
            mixed = _sample_out(jnp.transpose(o_lat, (1, 0, 2)), wv_pairs, o_all, db)
            w_mix, mix_scale = w_o[j].astype(BF16), ones_d
            ckv_p.append(ckv_all[:n_prompt].reshape(batch, seq, KV_LORA))
            kr_p.append(kr_all[:n_prompt].reshape(batch, seq, QK_ROPE))
            ckv_s.append(ckv_all[n_prompt:].reshape(db, 1, KV_LORA))
            kr_s.append(kr_all[n_prompt:].reshape(db, 1, QK_ROPE))
        else:
            pooled = _pool_prompt(x, batch, seq, POOL_TILE)
            mixed = _pool_sample(jnp.transpose(state_pool[j], (1, 0, 2)).astype(F32), x, pooled, db)
            w_mix = w_pool[j].astype(BF16)
            mix_scale = row(pool_scale[j])
            pool_p.append(jnp.stack([x[(b + 1) * seq - POOL_CTX:(b + 1) * seq] for b in range(batch)]))
            pool_s.append(jnp.concatenate([state_pool[j][:, 1:].astype(F32), x[n_prompt:, None, :]], 1))
        xe, cls, rank, counts = _post_mix(
            mixed, x, w_mix, mix_scale, row(ln1_g[i]), row(ln1_b[i]), wr_hi, wr_lo, rbias, tri, alpha, tm)
        row_pos, tile_a, tile_b, n_used = _dispatch_tables(
            cls.reshape(-1), rank.reshape(-1), counts.reshape(-1).astype(jnp.int32), n_moe_tiles)
        xs = _dispatch(row_pos, xe, xs, tm_dispatch)
        ys = _moe(tile_a, tile_b, n_used, xs, w_gate, w_up, w_down, i,
                  row(ln2_g[i]), row(ln2_b[i]), alpha, MOE_TILE)
        x = _unpermute(row_pos, ys, n, d, tm)
    return (x[:n_prompt].reshape(batch, seq, d), x[n_prompt:].reshape(db, 1, d),
            jnp.stack(ckv_p), jnp.stack(kr_p), jnp.stack(pool_p),
            jnp.stack(ckv_s), jnp.stack(kr_s), jnp.stack(pool_s))
```

```python
import functools
import math

import jax
import jax.numpy as jnp
from jax import lax
from jax.experimental import pallas as pl
from jax.experimental.pallas import tpu as pltpu

F32 = jnp.float32
BF16 = jnp.bfloat16

N_HEADS = 16
QK_NOPE = 64
QK_ROPE = 32
V_HEAD = 64
KV_LORA = 256
ROPE_THETA = 10000.0
ATTN_SCALE = (QK_NOPE + QK_ROPE) ** -0.5
POOL_WINDOWS = (2, 4, 8, 16)
POOL_CTX = max(POOL_WINDOWS) - 1
N_EXPERTS = 16
N_EXPERT_GROUPS = 4
EXPERTS_PER_GROUP = N_EXPERTS // N_EXPERT_GROUPS
TOP_K = 2
LN_EPS = 1e-5
RMS_EPS = 1e-6

PAIR_A = (0, 0, 0, 1, 1, 3)
PAIR_B = (1, 2, 3, 3, 2, 2)
N_PAIRS = len(PAIR_A)
N_CLASSES = N_EXPERT_GROUPS * N_PAIRS

LANES = 128
HALF_ROPE = QK_ROPE // 2
ROPE_LANE0 = QK_NOPE
HALO = 16
NEG_BIG = -1e30
LOG2E = math.log2(math.e)
VMEM_LIMIT = 52 * 1024 * 1024
MOE_TILE = 256
DISPATCH_TILE_MAX = 1408
ATTN_TILE = 512
ATTN_HEADS_PER_STEP = 4
ONES_ROWS = 16
POOL_TILE = 512
PAGES_PER_STEP = 16
SUBLANES = 8
EXT_ROWS = 2 * SUBLANES
EXT_USED = SUBLANES + 1
NT_DIMS = (((1,), (1,)), ((), ()))


def _params(*sem):
    return pltpu.CompilerParams(dimension_semantics=sem, vmem_limit_bytes=VMEM_LIMIT)


def _rms(x, g):
    return x * lax.rsqrt(jnp.mean(x * x, -1, keepdims=True) + RMS_EPS) * g


def _layer_norm(y, g, b):
    mu = jnp.mean(y, -1, keepdims=True)
    d = y - mu
    var = jnp.mean(d * d, -1, keepdims=True)
    return d * lax.rsqrt(var + LN_EPS) * g + b


def _load_chunks(ref, tokens, rows, n_chunks):
    return jnp.concatenate([ref[pl.ds(c, tokens, stride=rows), :] for c in range(n_chunks)], axis=1)


def _store_chunks(ref, val, rows):
    tokens, width = val.shape
    for c in range(width // LANES):
        ref[pl.ds(c, tokens, stride=rows), :] = val[:, c * LANES:(c + 1) * LANES]


def _rope_chunk(c, cos, sin_up, sin_dn):
    return c * cos + pltpu.roll(c, HALF_ROPE, 1) * sin_up + pltpu.roll(c, LANES - HALF_ROPE, 1) * sin_dn


def _mla_proj_kernel(x_ref, wdq_ref, gq_ref, wuq_ref, wdkv_ref, gkv_ref, wuk_ref, wuvt_ref,
                     cos_ref, sup_ref, sdn_ref, q_ref, k_ref, vt_ref, ckv_ref, kr_ref):
    xb = x_ref[...].astype(BF16)
    cos, sup, sdn = cos_ref[...], sup_ref[...], sdn_ref[...]
    cq = _rms(jnp.dot(xb, wdq_ref[...], preferred_element_type=F32), gq_ref[...])
    q = jnp.dot(cq.astype(BF16), wuq_ref[...], preferred_element_type=F32)
    kv = jnp.dot(xb, wdkv_ref[...], preferred_element_type=F32)
    ckv = _rms(kv[:, :KV_LORA], gkv_ref[...])
    ckv_ref[...] = ckv
    kr = _rope_chunk(kv[:, KV_LORA:], cos, sup, sdn)
    kr_ref[...] = kr
    ckv_b = ckv.astype(BF16)
    kn = jnp.dot(ckv_b, wuk_ref[...], preferred_element_type=F32)
    vt_ref[...] = lax.dot_general(wuvt_ref[...], ckv_b, NT_DIMS, preferred_element_type=F32).astype(BF16)
    for h in range(N_HEADS):
        sl = slice(h * LANES, (h + 1) * LANES)
        q_ref[:, sl] = (_rope_chunk(q[:, sl], cos, sup, sdn) * (ATTN_SCALE * LOG2E)).astype(BF16)
        k_ref[:, sl] = (kn[:, sl] + kr).astype(BF16)


def _mla_proj(x, wdq, gq, wuq, wdkv, gkv, wuk, wuvt, cos, sup, sdn, tm):
    n, d = x.shape
    full = lambda a: pl.BlockSpec(a.shape, lambda i: (0,) * a.ndim)
    rows = lambda w: pl.BlockSpec((tm, w), lambda i: (i, 0))
    hp = N_HEADS * LANES
    hv = N_HEADS * V_HEAD
    return pl.pallas_call(
        _mla_proj_kernel,
        grid=(n // tm,),
        in_specs=[rows(d), full(wdq), full(gq), full(wuq), full(wdkv), full(gkv), full(wuk), full(wuvt),
                  rows(LANES), rows(LANES), rows(LANES)],
        out_specs=[rows(hp), rows(hp), pl.BlockSpec((hv, tm), lambda i: (0, i)), rows(KV_LORA), rows(LANES)],
        out_shape=[jax.ShapeDtypeStruct((n, hp), BF16), jax.ShapeDtypeStruct((n, hp), BF16),
                   jax.ShapeDtypeStruct((hv, n), BF16),
                   jax.ShapeDtypeStruct((n, KV_LORA), F32), jax.ShapeDtypeStruct((n, LANES), F32)],
        compiler_params=_params("parallel"),
        name="mla_proj",
    )(x, wdq, gq, wuq, wdkv, gkv, wuk, wuvt, cos, sup, sdn)


def _flash_kernel(q_ref, k_ref, vt_ref, o_ref, *, tile, heads):
    qi = pl.program_id(2)
    krow = lax.broadcasted_iota(jnp.int32, (tile, tile), 0)
    qcol = lax.broadcasted_iota(jnp.int32, (tile, tile), 1)
    ones = jnp.ones((ONES_ROWS, tile), BF16)

    def step(kv, carry, masked):
        start = pl.multiple_of(kv * tile, tile)
        new = []
        scores = [lax.dot_general(k_ref[pl.ds(start, tile), hh * LANES:(hh + 1) * LANES],
                                  q_ref[:, hh * LANES:(hh + 1) * LANES], NT_DIMS, preferred_element_type=F32)
                  for hh in range(heads)]
        for hh in range(heads):
            m, acc = carry[hh]
            s = scores[hh]
            if masked:
                s = jnp.where(krow <= qcol, s, NEG_BIG)
            m_new = jnp.maximum(m, jnp.max(s, 0, keepdims=True))
            p = jnp.exp2(s - m_new).astype(BF16)
            a = jnp.exp2(m - m_new)
            vt = jnp.concatenate([vt_ref[hh * V_HEAD:(hh + 1) * V_HEAD, pl.ds(start, tile)], ones], axis=0)
            acc = a * acc + jnp.dot(vt, p, preferred_element_type=F32)
            new.append((m_new, acc))
        return tuple(new)

    init = tuple((jnp.full((1, tile), NEG_BIG, F32), jnp.zeros((V_HEAD + ONES_ROWS, tile), F32))
                 for _ in range(heads))
    carry = lax.fori_loop(0, qi, lambda kv, c: step(kv, c, False), init)
    carry = step(qi, carry, True)
    o_t = jnp.concatenate([acc[:V_HEAD] / acc[V_HEAD:V_HEAD + 1] for _, acc in carry], axis=0)
    o_ref[...] = o_t.T.astype(BF16)


def _flash_attention(q, k, vt, batch, seq, tile):
    n = q.shape[0]
    nq = seq // tile
    hps = ATTN_HEADS_PER_STEP
    return pl.pallas_call(
        functools.partial(_flash_kernel, tile=tile, heads=hps),
        grid=(batch, N_HEADS // hps, nq),
        in_specs=[pl.BlockSpec((tile, hps * LANES), lambda b, p, i: (b * nq + i, p)),
                  pl.BlockSpec((seq, hps * LANES), lambda b, p, i: (b, p)),
                  pl.BlockSpec((hps * V_HEAD, seq), lambda b, p, i: (p, b))],
        out_specs=pl.BlockSpec((tile, hps * V_HEAD), lambda b, p, i: (b * nq + i, p)),
        out_shape=jax.ShapeDtypeStruct((n, N_HEADS * V_HEAD), BF16),
        compiler_params=_params("parallel", "parallel", "arbitrary"),
        name="flash_prompt",
    )(q, k, vt)


def _qlat_kernel(q_ref, w_ref, o_ref):
    o_ref[...] = jnp.dot(q_ref[...], w_ref[...], preferred_element_type=F32).astype(BF16)


def _q_latent(q, w_ukt, db):
    n = q.shape[0]
    blk = n // db - 1
    return pl.pallas_call(
        _qlat_kernel,
        grid=(N_HEADS,),
        in_specs=[pl.BlockSpec((db, LANES), lambda h: (blk, h)),
                  pl.BlockSpec((None, LANES, KV_LORA), lambda h: (h, 0, 0))],
        out_specs=pl.BlockSpec((None, db, KV_LORA), lambda h: (h, 0, 0)),
        out_shape=jax.ShapeDtypeStruct((N_HEADS, db, KV_LORA), BF16),
        compiler_params=_params("parallel"),
        name="q_latent",
    )(q, w_ukt)


def _paged_kernel(pt_ref, ql_ref, qr_ref, cnew_ref, knew_ref, ckv_hbm, krt_hbm, o_ref, kbuf, rbuf, sem,
                  *, layer, n_pages, psz, pps):
    b = pl.program_id(0)

    def page_copies(seq, slot):
        for i in range(n_pages):
            page = pt_ref[seq * n_pages + i]
            yield pltpu.make_async_copy(ckv_hbm.at[layer, page], kbuf.at[slot, pl.ds(i * psz, psz), :],
                                        sem.at[slot])
            yield pltpu.make_async_copy(krt_hbm.at[layer, page], rbuf.at[slot, :, pl.ds(i * psz, psz)],
                                        sem.at[slot])

    @pl.when(b == 0)
    def _():
        for cp in page_copies(0, 0):
            cp.start()

    @pl.when(b + 1 < pl.num_programs(0))
    def _():
        for cp in page_copies(b + 1, (b + 1) % 2):
            cp.start()

    slot = b % 2
    for cp in page_copies(b, slot):
        cp.wait()

    ql = ql_ref[...]
    qr = qr_ref[...]
    width = pps * psz
    chunks = range(n_pages // pps)
    keys = [kbuf[slot, c * width:(c + 1) * width, :].astype(BF16) for c in chunks]
    krts = [rbuf[slot, :, c * width:(c + 1) * width].astype(BF16) for c in chunks]
    scores = [lax.dot_general(ql, keys[c], NT_DIMS, preferred_element_type=F32)
              + jnp.dot(qr, krts[c], preferred_element_type=F32) for c in chunks]
    maxes = [jnp.max(s, -1, keepdims=True) for s in scores]
    probs = [jnp.exp2(scores[c] - maxes[c]).astype(BF16) for c in chunks]
    parts = [(maxes[c], jnp.sum(probs[c].astype(F32), -1, keepdims=True),
              jnp.dot(probs[c], keys[c], preferred_element_type=F32)) for c in chunks]
    cn = cnew_ref[...].astype(BF16).astype(F32)
    kn = knew_ref[...].astype(BF16).astype(F32)
    s_self = (jnp.sum(ql.astype(F32) * cn, -1, keepdims=True) + jnp.sum(qr.astype(F32) * kn, -1, keepdims=True))
    m = functools.reduce(jnp.maximum, [pt[0] for pt in parts] + [s_self])
    p_self = jnp.exp2(s_self - m).astype(BF16).astype(F32)
    l = p_self
    acc = p_self * cn
    for m_c, l_c, acc_c in parts:
        a = jnp.exp2(m_c - m)
        l = l + a * l_c
        acc = acc + a * acc_c
    o_ref[...] = acc / l


def _paged_attention(page_table, q_lat, q_rope, c_new, k_new, cache_ckv, cache_krt, layer, pps):
    db, n_pages = page_table.shape
    psz = cache_ckv.shape[2]
    pt = page_table.reshape(-1)
    per_b = lambda r, w: pl.BlockSpec((None, r, w), lambda b, pt_ref: (b, 0, 0))
    grid_spec = pltpu.PrefetchScalarGridSpec(
        num_scalar_prefetch=1,
        grid=(db,),
        in_specs=[per_b(N_HEADS, KV_LORA), per_b(N_HEADS, QK_ROPE), per_b(1, KV_LORA), per_b(1, QK_ROPE),
                  pl.BlockSpec(memory_space=pl.ANY), pl.BlockSpec(memory_space=pl.ANY)],
        out_specs=per_b(N_HEADS, KV_LORA),
        scratch_shapes=[pltpu.VMEM((2, n_pages * psz, KV_LORA), cache_ckv.dtype),
                        pltpu.VMEM((2, QK_ROPE, n_pages * psz), cache_krt.dtype),
                        pltpu.SemaphoreType.DMA((2,))],
    )
    return pl.pallas_call(
        functools.partial(_paged_kernel, layer=layer, n_pages=n_pages, psz=psz, pps=pps),
        grid_spec=grid_spec,
        out_shape=jax.ShapeDtypeStruct((db, N_HEADS, KV_LORA), F32),
        compiler_params=_params("arbitrary"),
        name="paged_sample",
    )(pt, q_lat, q_rope, c_new, k_new, cache_ckv, cache_krt)


def _sample_out_kernel(ol_ref, w_ref, oin_ref, o_ref):
    del oin_ref
    acc = jnp.dot(ol_ref[0].astype(BF16), w_ref[0], preferred_element_type=F32)
    acc = acc + jnp.dot(ol_ref[1].astype(BF16), w_ref[1], preferred_element_type=F32)
    o_ref[...] = acc.astype(BF16)


def _sample_out(o_lat_t, w_uv_pairs, o_all, db):
    n = o_all.shape[0]
    blk = n // db - 1
    return pl.pallas_call(
        _sample_out_kernel,
        grid=(N_HEADS // 2,),
        in_specs=[pl.BlockSpec((2, db, KV_LORA), lambda p: (p, 0, 0)),
                  pl.BlockSpec((2, KV_LORA, 2 * V_HEAD), lambda p: (p, 0, 0)),
                  pl.BlockSpec(memory_space=pl.ANY)],
        out_specs=pl.BlockSpec((db, 2 * V_HEAD), lambda p: (blk, p)),
        out_shape=jax.ShapeDtypeStruct(o_all.shape, o_all.dtype),
        input_output_aliases={2: 0},
        compiler_params=_params("parallel"),
        name="sample_out",
    )(o_lat_t, w_uv_pairs, o_all)


def _pool_prompt_kernel(x_ref, halo_ref, o_ref, l0, l1, l2, l3, *, tile, seq):
    i = pl.program_id(0)
    d = x_ref.shape[1]
    gw = d // len(POOL_WINDOWS)
    ext = tile + HALO
    s0 = (i * tile) % seq
    pos_ext = s0 - HALO + lax.broadcasted_iota(jnp.int32, (ext, 1), 0)
    zeros = jnp.zeros((HALO, d), F32)
    for buf in (l0, l1, l2, l3):
        buf[0:HALO, :] = zeros
    l0[HALO:2 * HALO, :] = halo_ref[...]
    l0[2 * HALO:, :] = x_ref[...]

    def level(src, dst, k, c0):
        cur = src[HALO:HALO + ext, c0:]
        sh = src[HALO - k:HALO - k + ext, c0:]
        dst[HALO:HALO + ext, c0:] = cur + jnp.where(pos_ext >= k, sh, 0.0)

    level(l0, l1, 1, 0)
    level(l1, l2, 2, gw)
    level(l2, l3, 4, 2 * gw)
    pos = pos_ext[HALO:]
    x = x_ref[...]
    t0 = 2 * HALO
    sums = (l1[t0:, 0:gw], l2[t0:, gw:2 * gw], l3[t0:, 2 * gw:3 * gw],
            l3[t0:, 3 * gw:] + jnp.where(pos >= 8, l3[t0 - 8:t0 - 8 + tile, 3 * gw:], 0.0))
    for g, w in enumerate(POOL_WINDOWS):
        cnt = jnp.minimum(pos + 1, w).astype(F32)
        o_ref[:, g * gw:(g + 1) * gw] = (sums[g] / cnt - x[:, g * gw:(g + 1) * gw]).astype(BF16)


def _pool_prompt(x, batch, seq, tile):
    n, d = x.shape
    hb = tile // HALO
    return pl.pallas_call(
        functools.partial(_pool_prompt_kernel, tile=tile, seq=seq),
        grid=(batch * seq // tile,),
        in_specs=[pl.BlockSpec((tile, d), lambda i: (i, 0)),
                  pl.BlockSpec((HALO, d), lambda i: (jnp.maximum(i * hb - 1, 0), 0))],
        out_specs=pl.BlockSpec((tile, d), lambda i: (i, 0)),
        out_shape=jax.ShapeDtypeStruct((n, d), BF16),
        scratch_shapes=[pltpu.VMEM((tile + 2 * HALO, d), F32)] * 4,
        compiler_params=_params("parallel"),
        name="pool_prompt",
    )(x, x)


def _pool_sample_kernel(st_ref, x_ref, pin_ref, o_ref):
    del pin_ref
    d = x_ref.shape[1]
    gw = d // len(POOL_WINDOWS)
    x = x_ref[...]
    for g, w in enumerate(POOL_WINDOWS):
        cs = slice(g * gw, (g + 1) * gw)
        acc = x[:, cs]
        for r in range(POOL_CTX - (w - 1), POOL_CTX):
            acc = acc + st_ref[r, :, cs]
        o_ref[:, cs] = (acc / float(w) - x[:, cs]).astype(BF16)


def _pool_sample(state_t, x, pooled, db):
    n, d = x.shape
    blk = n // db - 1
    return pl.pallas_call(
        _pool_sample_kernel,
        grid=(1,),
        in_specs=[pl.BlockSpec(state_t.shape, lambda i: (0, 0, 0)),
                  pl.BlockSpec((db, d), lambda i: (blk, 0)),
                  pl.BlockSpec(memory_space=pl.ANY)],
        out_specs=pl.BlockSpec((db, d), lambda i: (blk, 0)),
        out_shape=jax.ShapeDtypeStruct(pooled.shape, pooled.dtype),
        input_output_aliases={2: 0},
        compiler_params=_params("arbitrary"),
        name="pool_sample",
    )(state_t, x, pooled)


def _route_rows(logits_t, bias_ref):
    s = 1.0 / (1.0 + jnp.exp(-logits_t))
    sr = [s[e:e + 1, :] for e in range(N_EXPERTS)]
    br = [sr[e] + bias_ref[e:e + 1, :] for e in range(N_EXPERTS)]
    gscore = []
    for g in range(N_EXPERT_GROUPS):
        r = br[g * EXPERTS_PER_GROUP:(g + 1) * EXPERTS_PER_GROUP]
        best2 = None
        for a in range(EXPERTS_PER_GROUP):
            for b in range(a + 1, EXPERTS_PER_GROUP):
                pair = r[a] + r[b]
                best2 = pair if best2 is None else jnp.maximum(best2, pair)
        gscore.append(best2)
    top = functools.reduce(jnp.maximum, gscore)
    best = jnp.full(top.shape, N_EXPERT_GROUPS - 1, jnp.int32)
    for g in range(N_EXPERT_GROUPS - 2, -1, -1):
        best = jnp.where(gscore[g] == top, g, best)
    sel = []
    for e in range(N_EXPERTS):
        g = e // EXPERTS_PER_GROUP
        ahead = jnp.zeros(top.shape, F32)
        for o in range(g * EXPERTS_PER_GROUP, (g + 1) * EXPERTS_PER_GROUP):
            if o == e:
                continue
            beats = (br[o] >= br[e]) if o < e else (br[o] > br[e])
            ahead = ahead + jnp.where(beats, 1.0, 0.0)
        sel.append(jnp.where(ahead < float(TOP_K), 1.0, 0.0) * jnp.where(best == g, 1.0, 0.0))
    den = functools.reduce(lambda a, b: a + b, [sel[e] * sr[e] for e in range(N_EXPERTS)])
    wts = [sel[e] * sr[e] / den for e in range(N_EXPERTS)]
    return sel, wts


def _post_mix_kernel(a_ref, x_ref, w_ref, sc_ref, g_ref, b_ref, wrh_ref, wrl_ref, rb_ref, tri_ref,
                     xe_ref, cls_ref, rank_ref, cnt_ref, cnt_scr, *, alpha):
    i = pl.program_id(0)
    d = x_ref.shape[1]

    @pl.when(i == 0)
    def _():
        cnt_scr[...] = jnp.zeros(cnt_scr.shape, F32)

    if len(w_ref.shape) == 3:
        gw = w_ref.shape[1]
        mix = jnp.concatenate([jnp.dot(a_ref[:, g * gw:(g + 1) * gw], w_ref[g], preferred_element_type=F32)
                               for g in range(w_ref.shape[0])], axis=1)
    else:
        mix = jnp.dot(a_ref[...], w_ref[...], preferred_element_type=F32)
    x1 = _layer_norm(alpha * x_ref[...] + mix * sc_ref[...], g_ref[...], b_ref[...])
    _store_chunks(xe_ref, x1, EXT_ROWS)
    x_hi = x1.astype(BF16)
    x_lo = (x1 - x_hi.astype(F32)).astype(BF16)
    wrh = wrh_ref[...]
    logits_t = (lax.dot_general(wrh, x_hi, NT_DIMS, preferred_element_type=F32)
                + lax.dot_general(wrh, x_lo, NT_DIMS, preferred_element_type=F32)
                + lax.dot_general(wrl_ref[...], x_hi, NT_DIMS, preferred_element_type=F32))
    sel, wts = _route_rows(logits_t, rb_ref)
    t = logits_t.shape[1]
    ind = []
    w_a = jnp.zeros(sel[0].shape, F32)
    w_b = jnp.zeros(sel[0].shape, F32)
    cls = jnp.zeros(sel[0].shape, F32)
    for g in range(N_EXPERT_GROUPS):
        for k in range(N_PAIRS):
            ea = g * EXPERTS_PER_GROUP + PAIR_A[k]
            eb = g * EXPERTS_PER_GROUP + PAIR_B[k]
            hit = sel[ea] * sel[eb]
            w_a = w_a + hit * wts[ea]
            w_b = w_b + hit * wts[eb]
            cls = cls + hit * float(len(ind))
            ind.append(hit)
    crow = lax.broadcasted_iota(jnp.int32, (N_CLASSES, t), 0)
    ind_t = jnp.zeros((N_CLASSES, t), F32)
    for c in range(N_CLASSES):
        ind_t = jnp.where(crow == c, ind[c], ind_t)
    before = jnp.dot(ind_t.astype(BF16), tri_ref[...], preferred_element_type=F32)
    rank_t = cnt_scr[...] + before
    cnt_new = cnt_scr[...] + jnp.sum(ind_t, -1, keepdims=True)
    cnt_scr[...] = cnt_new
    cnt_ref[...] = cnt_new
    rank = jnp.zeros(sel[0].shape, F32)
    for c in range(N_CLASSES):
        rank = rank + ind[c] * rank_t[c:c + 1, :]
    cls_ref[...] = cls.astype(jnp.int32)
    rank_ref[...] = rank.astype(jnp.int32)
    lrow = lax.broadcasted_iota(jnp.int32, (LANES, t), 0)
    w_t = jnp.where(lrow == 0, w_a, jnp.where(lrow == 1, w_b, 0.0))
    xe_ref[pl.ds(EXT_USED - 1, t, stride=EXT_ROWS), :] = w_t.T


def _post_mix(a, x, w, scale, g, b, wr_hi, wr_lo, rbias, tri, alpha, tm):
    n, d = x.shape
    full = lambda t: pl.BlockSpec(t.shape, lambda i: (0,) * t.ndim)
    rows = lambda wd: pl.BlockSpec((tm, wd), lambda i: (i, 0))
    cols = pl.BlockSpec((1, tm), lambda i: (0, i))
    return pl.pallas_call(
        functools.partial(_post_mix_kernel, alpha=alpha),
        grid=(n // tm,),
        in_specs=[rows(d), rows(d), full(w), full(scale), full(g), full(b), full(wr_hi), full(wr_lo),
                  full(rbias), full(tri)],
        out_specs=[pl.BlockSpec((tm * EXT_ROWS, LANES), lambda i: (i, 0)), cols, cols,
                   pl.BlockSpec((N_CLASSES, 1), lambda i: (0, 0))],
        out_shape=[jax.ShapeDtypeStruct((n * EXT_ROWS, LANES), F32),
                   jax.ShapeDtypeStruct((1, n), jnp.int32), jax.ShapeDtypeStruct((1, n), jnp.int32),
                   jax.ShapeDtypeStruct((N_CLASSES, 1), F32)],
        scratch_shapes=[pltpu.VMEM((N_CLASSES, 1), F32)],
        compiler_params=_params("arbitrary"),
        name="post_mix",
    )(a, x, w, scale, g, b, wr_hi, wr_lo, rbias, tri)


def _token_copy(src_ref, src_tok, dst_ref, dst_tok, rows, sem, used=None):
    used = rows if used is None else used
    src = src_ref.at[pl.ds(pl.multiple_of(src_tok * rows, rows), used), :]
    dst = dst_ref.at[pl.ds(pl.multiple_of(dst_tok * rows, rows), used), :]
    return pltpu.make_async_copy(src, dst, sem)


def _wait_tokens(src_ref, dst_ref, tokens, rows, sem):
    span = pl.ds(0, tokens * rows)
    pltpu.make_async_copy(src_ref.at[span, :], dst_ref.at[span, :], sem).wait()


def _dispatch_kernel(pos_ref, x_ref, xs_in_ref, xs_ref, sem, *, tm):
    del xs_in_ref
    base = pl.program_id(0) * tm

    def pair(r2, carry):
        for u in range(2):
            r = 2 * r2 + u
            _token_copy(x_ref, r, xs_ref, pos_ref[base + r], EXT_ROWS, sem, used=EXT_USED).start(priority=u)
        return carry

    lax.fori_loop(0, tm // 2, pair, 0, unroll=4)
    _wait_tokens(x_ref, xs_ref, tm, EXT_USED, sem)


def _dispatch(pos, xe, xs_init, tm):
    n = xe.shape[0] // EXT_ROWS
    assert tm % 2 == 0
    grid_spec = pltpu.PrefetchScalarGridSpec(
        num_scalar_prefetch=1,
        grid=(n // tm,),
        in_specs=[pl.BlockSpec((tm * EXT_ROWS, LANES), lambda i, p: (i, 0)), pl.BlockSpec(memory_space=pl.ANY)],
        out_specs=pl.BlockSpec(memory_space=pl.ANY),
        scratch_shapes=[pltpu.SemaphoreType.DMA(())],
    )
    return pl.pallas_call(
        functools.partial(_dispatch_kernel, tm=tm),
        grid_spec=grid_spec,
        out_shape=jax.ShapeDtypeStruct(xs_init.shape, xs_init.dtype),
        input_output_aliases={2: 0},
        compiler_params=_params("arbitrary"),
        name="moe_dispatch",
    )(pos, xe, xs_init)


def _unpermute_kernel(pos_ref, xs_ref, o_ref, buf, sem, *, tm):
    i = pl.program_id(0)
    rows = o_ref.shape[1] // LANES

    def issue(tile, slot):
        base = tile * tm

        def pair(r2, carry):
            for u in range(2):
                r = 2 * r2 + u
                _token_copy(xs_ref, pos_ref[base + r], buf.at[slot], r, rows, sem.at[slot]).start(priority=u)
            return carry

        lax.fori_loop(0, tm // 2, pair, 0, unroll=4)

    @pl.when(i == 0)
    def _():
        issue(0, 0)

    @pl.when(i + 1 < pl.num_programs(0))
    def _():
        issue(i + 1, (i + 1) % 2)

    slot = i % 2
    _wait_tokens(xs_ref, buf.at[slot], tm, rows, sem.at[slot])
    o_ref[...] = _load_chunks(buf.at[slot], tm, rows, rows)


def _unpermute(pos, xs, n, d, tm):
    rows = d // LANES
    assert tm % 2 == 0
    grid_spec = pltpu.PrefetchScalarGridSpec(
        num_scalar_prefetch=1,
        grid=(n // tm,),
        in_specs=[pl.BlockSpec(memory_space=pl.ANY)],
        out_specs=pl.BlockSpec((tm, d), lambda i, p: (i, 0)),
        scratch_shapes=[pltpu.VMEM((2, tm * rows, LANES), xs.dtype), pltpu.SemaphoreType.DMA((2,))],
    )
    return pl.pallas_call(
        functools.partial(_unpermute_kernel, tm=tm),
        grid_spec=grid_spec,
        out_shape=jax.ShapeDtypeStruct((n, d), xs.dtype),
        compiler_params=_params("arbitrary"),
        name="moe_unpermute",
    )(pos, xs)


def _moe_kernel(ta_ref, tb_ref, nu_ref, tv_ref, x_ref, wga_ref, wua_ref, wda_ref, wgb_ref, wub_ref, wdb_ref,
                g_ref, b_ref, o_ref, wgu_s, wd_s, *, alpha, tile):
    i = pl.program_id(0)
    n_chunks = o_ref.shape[0] // tile
    f = wga_ref.shape[1]

    @pl.when(i < nu_ref[0])
    def _():
        prev = jnp.maximum(i - 1, 0)

        @pl.when((i == 0) | (ta_ref[i] != ta_ref[prev]))
        def _():
            wgu_s[:, 0:f] = wga_ref[...].astype(BF16)
            wgu_s[:, f:2 * f] = wua_ref[...].astype(BF16)
            wd_s[0:f, :] = wda_ref[...].astype(BF16)

        @pl.when((i == 0) | (tb_ref[i] != tb_ref[prev]))
        def _():
            wgu_s[:, 2 * f:3 * f] = wgb_ref[...].astype(BF16)
            wgu_s[:, 3 * f:4 * f] = wub_ref[...].astype(BF16)
            wd_s[f:2 * f, :] = wdb_ref[...].astype(BF16)

        def compute(rows):
            x = _load_chunks(x_ref, rows, EXT_ROWS, n_chunks)
            combine = x_ref[pl.ds(n_chunks, rows, stride=EXT_ROWS), :]
            gu = jnp.dot(x.astype(BF16), wgu_s[...], preferred_element_type=F32)
            hs = []
            for col in range(TOP_K):
                gate = gu[:, 2 * col * f:(2 * col + 1) * f]
                up = gu[:, (2 * col + 1) * f:(2 * col + 2) * f]
                hs.append((gate * (1.0 / (1.0 + jnp.exp(-gate))) * up * combine[:, col:col + 1]).astype(BF16))
            y = jnp.dot(jnp.concatenate(hs, axis=1), wd_s[...], preferred_element_type=F32)
            _store_chunks(o_ref, _layer_norm(alpha * x + y, g_ref[...], b_ref[...]), n_chunks)

        quarter = tile // 4
        for k in range(1, 5):
            @pl.when((tv_ref[i] > (k - 1) * quarter) & (tv_ref[i] <= k * quarter))
            def _(rows=k * quarter):
                compute(rows)


def _moe(tile_a, tile_b, n_used, tile_valid, xs, w_gate, w_up, w_down, layer, g, b, alpha, tile):
    p = xs.shape[0] // EXT_ROWS
    d, f = w_gate.shape[-2:]
    ch = d // LANES
    row_blk = lambda i, ta, tb, nu, tv: (jnp.minimum(i, nu[0] - 1), 0)
    wa = lambda r, c: pl.BlockSpec((None, None, r, c), lambda i, ta, tb, nu, tv: (layer, ta[i], 0, 0))
    wb = lambda r, c: pl.BlockSpec((None, None, r, c), lambda i, ta, tb, nu, tv: (layer, tb[i], 0, 0))
    full = lambda t: pl.BlockSpec(t.shape, lambda i, ta, tb, nu, tv: (0,) * t.ndim)
    grid_spec = pltpu.PrefetchScalarGridSpec(
        num_scalar_prefetch=4,
        grid=(p // tile,),
        in_specs=[pl.BlockSpec((tile * EXT_ROWS, LANES), row_blk), wa(d, f), wa(d, f), wa(f, d),
                  wb(d, f), wb(d, f), wb(f, d), full(g), full(b)],
        out_specs=pl.BlockSpec((tile * ch, LANES), row_blk),
        scratch_shapes=[pltpu.VMEM((d, 2 * TOP_K * f), BF16), pltpu.VMEM((TOP_K * f, d), BF16)],
    )
    return pl.pallas_call(
        functools.partial(_moe_kernel, alpha=alpha, tile=tile),
        grid_spec=grid_spec,
        out_shape=jax.ShapeDtypeStruct((p * ch, LANES), F32),
        compiler_params=_params("arbitrary"),
        name="moe_grouped",
    )(tile_a, tile_b, n_used, tile_valid, xs, w_gate, w_up, w_down, w_gate, w_up, w_down, g, b)


def _dispatch_tables(cls, rank, counts, n_tiles):
    padded = ((counts + MOE_TILE - 1) // MOE_TILE) * MOE_TILE
    ends = jnp.cumsum(padded)
    pos = (ends - padded)[cls] + rank
    tile_start = jnp.arange(n_tiles, dtype=jnp.int32) * MOE_TILE
    tile_cls = jnp.minimum(jnp.sum((tile_start[:, None] >= ends[None, :]).astype(jnp.int32), 1), N_CLASSES - 1)
    group0 = (tile_cls // N_PAIRS) * EXPERTS_PER_GROUP
    tile_a = group0 + jnp.asarray(PAIR_A, jnp.int32)[tile_cls % N_PAIRS]
    tile_b = group0 + jnp.asarray(PAIR_B, jnp.int32)[tile_cls % N_PAIRS]
    n_used = (ends[-1:] // MOE_TILE).astype(jnp.int32)
    tile_valid = jnp.clip((ends - padded + counts)[tile_cls] - tile_start, 0, MOE_TILE)
    return (pos.astype(jnp.int32), tile_a.astype(jnp.int32), tile_b.astype(jnp.int32), n_used,
            tile_valid.astype(jnp.int32))


def _rope_tables(pos):
    inv = ROPE_THETA ** (-2.0 * jnp.arange(HALF_ROPE, dtype=F32) / QK_ROPE)
    ang = pos.astype(F32)[:, None] * inv[None, :]
    cos, sin = jnp.cos(ang), jnp.sin(ang)
    n = pos.shape[0]
    ones = jnp.ones((n, ROPE_LANE0), F32)
    zeros = jnp.zeros((n, ROPE_LANE0), F32)
    tail1 = jnp.ones((n, LANES - ROPE_LANE0 - QK_ROPE), F32)
    tail0 = jnp.zeros((n, LANES - ROPE_LANE0 - QK_ROPE), F32)
    zh = jnp.zeros((n, HALF_ROPE), F32)
    cos_t = jnp.concatenate([ones, cos, cos, tail1], 1)
    sin_up = jnp.concatenate([zeros, zh, sin, tail0], 1)
    sin_dn = jnp.concatenate([zeros, -sin, zh, tail0], 1)
    return cos_t, sin_up, sin_dn


def _pad_heads(w, per_head):
    k = w.shape[0]
    w = w.reshape(k, N_HEADS, per_head)
    return jnp.pad(w, ((0, 0), (0, 0), (0, LANES - per_head))).reshape(k, N_HEADS * LANES)


def kernel(x_prompt, x_sample, cache_ckv, cache_krope, state_pool, page_table, w_dq, g_q, w_uq, w_dkv, g_kv,
           w_uk, w_uv, w_o, w_pool, pool_scale, w_router, router_bias, w_gate, w_up, w_down,
           ln1_g, ln1_b, ln2_g, ln2_b):
    batch, seq, d = x_prompt.shape
    db, dec_t, _ = x_sample.shape
    assert dec_t == 1 and d % LANES == 0 and seq % ATTN_TILE == 0 and seq % POOL_TILE == 0
    depth = ln1_g.shape[0]
    alpha = (2 * depth) ** 0.25
    n_prompt = batch * seq
    n = n_prompt + db
    assert n % db == 0 and db % 8 == 0
    tm = 384 if n % 384 == 0 else 128
    assert n % tm == 0
    tm_dispatch = max(t for t in range(2, DISPATCH_TILE_MAX + 1, 2) if n % t == 0)
    n_pages = page_table.shape[1]
    past_len = n_pages * cache_ckv.shape[2]
    pps = min(PAGES_PER_STEP, n_pages)
    assert n_pages % pps == 0
    n_moe_tiles = (n + N_CLASSES * (MOE_TILE - 1)) // MOE_TILE
    row = lambda v: v.reshape(1, -1).astype(F32)

    tables = _rope_tables(jnp.concatenate([jnp.arange(seq), jnp.full((1,), past_len)]))
    cos_t, sin_up, sin_dn = [jnp.concatenate([jnp.tile(t[:seq], (batch, 1)), jnp.tile(t[seq:], (db, 1))])
                             for t in tables]
    tri = (jnp.arange(tm)[:, None] < jnp.arange(tm)[None, :]).astype(BF16)
    wr_t = w_router.T.astype(F32)
    wr_hi = wr_t.astype(BF16)
    wr_lo = (wr_t - wr_hi.astype(F32)).astype(BF16)
    rbias = router_bias.reshape(N_EXPERTS, 1).astype(F32)
    ones_d = jnp.ones((1, d), F32)
    cache_krt = jnp.swapaxes(cache_krope, 2, 3)
    assert d // LANES + 1 == EXT_USED
    xs = jnp.zeros((n_moe_tiles * MOE_TILE * EXT_ROWS, LANES), F32)

    x = jnp.concatenate([x_prompt.reshape(n_prompt, d), x_sample.reshape(db, d)], 0)
    ckv_p, kr_p, pool_p, ckv_s, kr_s, pool_s = [], [], [], [], [], []
    for i in range(depth):
        j = i // 2
        if i % 2 == 0:
            wuq_p = _pad_heads(w_uq[j], QK_NOPE + QK_ROPE).astype(BF16)
            wdkv_p = jnp.concatenate(
                [w_dkv[j][:, :KV_LORA], jnp.zeros((d, ROPE_LANE0), F32), w_dkv[j][:, KV_LORA:],
                 jnp.zeros((d, LANES - ROPE_LANE0 - QK_ROPE), F32)], 1).astype(BF16)
            wuk_p = _pad_heads(w_uk[j].reshape(KV_LORA, N_HEADS * QK_NOPE), QK_NOPE).astype(BF16)
            wuv_t = w_uv[j].reshape(KV_LORA, N_HEADS * V_HEAD).T.astype(BF16)
            q_all, k_all, vt_all, ckv_all, kr_all = _mla_proj(
                x, w_dq[j].astype(BF16), row(g_q[j]), wuq_p, wdkv_p, row(g_kv[j]), wuk_p, wuv_t,
                cos_t, sin_up, sin_dn, tm)
            kr_all = kr_all[:, ROPE_LANE0:ROPE_LANE0 + QK_ROPE]
            o_all = _flash_attention(q_all, k_all, vt_all, batch, seq, ATTN_TILE)
            w_ukt = jnp.pad(jnp.transpose(w_uk[j], (1, 2, 0)), ((0, 0), (0, LANES - QK_NOPE), (0, 0))).astype(BF16)
            q_lat = jnp.transpose(_q_latent(q_all, w_ukt, db), (1, 0, 2))
            q_rope = q_all[n_prompt:].reshape(db, N_HEADS, LANES)[:, :, ROPE_LANE0:ROPE_LANE0 + QK_ROPE]
            o_lat = _paged_attention(page_table, q_lat, q_rope, ckv_all[n_prompt:].reshape(db, 1, KV_LORA),
                                     kr_all[n_prompt:].reshape(db, 1, QK_ROPE), cache_ckv, cache_krt, j, pps)
            wv = jnp.transpose(w_uv[j], (1, 0, 2))
            wv_even = jnp.pad(wv, ((0, 0), (0, 0), (0, V_HEAD)))
            wv_odd = jnp.pad(wv, ((0, 0), (0, 0), (V_HEAD, 0)))
            wv_pairs = jnp.where((jnp.arange(N_HEADS) % 2 == 0)[:, None, None], wv_even, wv_odd).astype(BF16)
            mixed = _sample_out(jnp.transpose(o_lat, (1, 0, 2)), wv_pairs, o_all, db)
            w_mix, mix_scale = w_o[j].astype(BF16), ones_d
            ckv_p.append(ckv_all[:n_prompt].reshape(batch, seq, KV_LORA))
            kr_p.append(kr_all[:n_prompt].reshape(batch, seq, QK_ROPE))
            ckv_s.append(ckv_all[n_prompt:].reshape(db, 1, KV_LORA))
            kr_s.append(kr_all[n_prompt:].reshape(db, 1, QK_ROPE))
        else:
            pooled = _pool_prompt(x, batch, seq, POOL_TILE)
            mixed = _pool_sample(jnp.transpose(state_pool[j], (1, 0, 2)).astype(F32), x, pooled, db)
            w_mix = w_pool[j].astype(BF16)
            mix_scale = row(pool_scale[j])
            pool_p.append(jnp.stack([x[(b + 1) * seq - POOL_CTX:(b + 1) * seq] for b in range(batch)]))
            pool_s.append(jnp.concatenate([state_pool[j][:, 1:].astype(F32), x[n_prompt:, None, :]], 1))
        xe, cls, rank, counts = _post_mix(
            mixed, x, w_mix, mix_scale, row(ln1_g[i]), row(ln1_b[i]), wr_hi, wr_lo, rbias, tri, alpha, tm)
        row_pos, tile_a, tile_b, n_used, tile_valid = _dispatch_tables(
            cls.reshape(-1), rank.reshape(-1), counts.reshape(-1).astype(jnp.int32), n_moe_tiles)
        xs = _dispatch(row_pos, xe, xs, tm_dispatch)
        ys = _moe(tile_a, tile_b, n_used, tile_valid, xs, w_gate, w_up, w_down, i,
                  row(ln2_g[i]), row(ln2_b[i]), alpha, MOE_TILE)
        x = _unpermute(row_pos, ys, n, d, tm)
    return (x[:n_prompt].reshape(batch, seq, d), x[n_prompt:].reshape(db, 1, d),
            jnp.stack(ckv_p), jnp.stack(kr_p), jnp.stack(pool_p),
            jnp.stack(ckv_s), jnp.stack(kr_s), jnp.stack(pool_s))
```
